```python
import math
import jax
import jax.numpy as jnp
from jax import lax
import numpy as np

D_MODEL = 1024
BATCH = 4
SEQ = 4096
DEPTH = 1

CHUNK = 64
QBLOCK = 128
SB_HEADS = 8
HEAD_DIM = 64
SB_WIDTH = SB_HEADS * HEAD_DIM
POOL_WINDOWS = (2, 4, 8, 16)
N_POOL_GROUPS = len(POOL_WINDOWS)
POOL_WIDTH = 512
POOL_GROUP_DIM = POOL_WIDTH // N_POOL_GROUPS
MAX_WIN = max(POOL_WINDOWS)
IN_WIDTH = 3 * SB_WIDTH + POOL_WIDTH + 2 * D_MODEL
N_EXPERTS = 64
EXPERT_DIM = 128
TOP_K = 8
N_EXPERT_GROUPS = 8
TOPK_GROUPS = 4
SHARED_DIM = 256
ROUTED_SCALE = 2.5
NORM_EPS = 1e-6

kernel_name = "hybrid_stickbreak_pool_moe_block"


def _rmsnorm(x, g):
    xf = x.astype(jnp.float32)
    r = lax.rsqrt(jnp.mean(xf * xf, axis=-1, keepdims=True) + NORM_EPS)
    return (xf * r).astype(x.dtype) * g


def _stick_breaking(q, k, v):
    S = q.shape[2]
    scale = 1.0 / math.sqrt(HEAD_DIM)
    outs = []
    for i0 in range(0, S, QBLOCK):
        kn = i0 + QBLOCK
        qb = q[:, :, i0:kn]
        kb = k[:, :, :kn]
        vb = v[:, :, :kn]
        z = jnp.einsum('bhqd,bhkd->bhqk', qb, kb).astype(jnp.float32) * scale
        qpos = i0 + jnp.arange(QBLOCK)[:, None]
        kpos = jnp.arange(kn)[None, :]
        mask = kpos < qpos
        log_fail = jnp.where(mask, jax.nn.log_sigmoid(-z), 0.0)
        between = lax.cumsum(log_fail, axis=3, reverse=True) - log_fail
        a = jnp.where(mask, jnp.exp(jax.nn.log_sigmoid(z) + between), 0.0)
        outs.append(jnp.einsum('bhqk,bhkd->bhqd', a.astype(vb.dtype), vb))
    return jnp.concatenate(outs, axis=2)


def _multiscale_pool(u, w_pool_group, pool_scale):
    B, S, _ = u.shape
    uf = u.astype(jnp.float32).reshape(B, S, N_POOL_GROUPS, POOL_GROUP_DIM)
    cs = jnp.pad(jnp.cumsum(uf, axis=1), ((0, 0), (MAX_WIN, 0), (0, 0), (0, 0)))
    cur = cs[:, MAX_WIN:]
    prev = jnp.stack([cs[:, MAX_WIN - w:MAX_WIN - w + S, gi]
                      for gi, w in enumerate(POOL_WINDOWS)], axis=2)
    count = jnp.minimum(jnp.arange(S)[:, None] + 1,
                        jnp.array(POOL_WINDOWS)[None, :]).astype(jnp.float32)
    mixed = (cur - prev) / count[None, :, :, None] - uf
    mixed = jnp.einsum('bsgc,gcd->bsgd', mixed.astype(u.dtype), w_pool_group)
    return mixed.reshape(B, S, POOL_WIDTH) * pool_scale


def _moe(h, w_router, router_bias, w_exp_gate, w_exp_up, w_exp_down,
         w_sh_gate, w_sh_up, w_sh_down):
    T = h.shape[0]
    per_group = N_EXPERTS // N_EXPERT_GROUPS
    scores = jax.nn.sigmoid((h @ w_router).astype(jnp.float32))
    sel = scores + router_bias.astype(jnp.float32)
    grp_score = jnp.sum(lax.top_k(sel.reshape(T, N_EXPERT_GROUPS, per_group), 2)[0], axis=-1)
    _, gidx = lax.top_k(grp_score, TOPK_GROUPS)
    gmask = jnp.any(gidx[:, :, None] == jnp.arange(N_EXPERT_GROUPS)[None, None, :], axis=1)
    emask = jnp.repeat(gmask, per_group, axis=1)
    _, eidx = lax.top_k(jnp.where(emask, sel, -jnp.inf), TOP_K)
    w = jnp.take_along_axis(scores, eidx, axis=1)
    w = w / jnp.sum(w, axis=-1, keepdims=True) * ROUTED_SCALE
    gates = jnp.zeros((T, N_EXPERTS), jnp.float32).at[jnp.arange(T)[:, None], eidx].set(w)
    a = jnp.einsum('td,edf->tef', h, w_exp_gate)
    b = jnp.einsum('td,edf->tef', h, w_exp_up)
    hmid = (jax.nn.silu(a) * b * gates[:, :, None]).astype(h.dtype)
    routed = jnp.einsum('tef,efd->td', hmid, w_exp_down)
    shared = (jax.nn.silu(h @ w_sh_gate) * (h @ w_sh_up)) @ w_sh_down
    return routed + shared


def _layer(x, c, w_ada, b_ada, g_norm1, w_in, g_q, g_k, w_sb_out, w_pool_group,
           pool_scale, w_pool_out, w_out, g_norm2, w_router, router_bias,
           w_exp_gate, w_exp_up, w_exp_down, w_sh_gate, w_sh_up, w_sh_down):
    B, S, D = x.shape
    mod = (jax.nn.silu(c) @ w_ada + b_ada)[:, None, :]
    shift1, scale1, gate1, shift2, scale2, gate2 = jnp.split(mod, 6, axis=-1)

    h = _rmsnorm(x, g_norm1) * (1 + scale1) + shift1
    proj = h @ w_in
    cuts = [SB_WIDTH, 2 * SB_WIDTH, 3 * SB_WIDTH, 3 * SB_WIDTH + POOL_WIDTH,
            3 * SB_WIDTH + POOL_WIDTH + D_MODEL]
    q, k, v, u, g_a, g_b = jnp.split(proj, cuts, axis=-1)

    def heads(t):
        return t.reshape(B, S, SB_HEADS, HEAD_DIM).transpose(0, 2, 1, 3)

    q = _rmsnorm(heads(q), g_q)
    k = _rmsnorm(heads(k), g_k)
    o_sb = _stick_breaking(q, k, heads(v)).transpose(0, 2, 1, 3).reshape(B, S, SB_WIDTH)
    y_sb = o_sb @ w_sb_out
    y_pool = _multiscale_pool(u, w_pool_group, pool_scale) @ w_pool_out
    merged = jax.nn.sigmoid(g_a) * y_sb + jax.nn.sigmoid(g_b) * y_pool
    x = x + gate1 * (merged @ w_out)

    h2 = _rmsnorm(x, g_norm2) * (1 + scale2) + shift2
    y = _moe(h2.reshape(B * S, D), w_router, router_bias, w_exp_gate, w_exp_up,
             w_exp_down, w_sh_gate, w_sh_up, w_sh_down).reshape(B, S, D)
    return x + gate2 * y


def setup_inputs(seed: int = 0) -> dict:
    key = jax.random.key(seed)
    ks = jax.random.split(key, 24)
    D, L = D_MODEL, DEPTH

    def nrm(k, shape, fan_in, mult=1.0):
        return jax.random.normal(k, shape, jnp.float32) * (mult * fan_in ** -0.5)

    def gain(k, shape):
        return 1.0 + 0.02 * jax.random.normal(k, shape, jnp.float32)

    return {
        "x": jax.random.normal(ks[0], (BATCH, SEQ, D), jnp.float32),
        "c": jax.random.normal(ks[1], (BATCH, D), jnp.float32),
        "w_ada": nrm(ks[2], (L, D, 6 * D), D, 0.5),
        "b_ada": 0.02 * jax.random.normal(ks[3], (L, 6 * D), jnp.float32),
        "g_norm1": gain(ks[4], (L, D)),
        "w_in": nrm(ks[5], (L, D, IN_WIDTH), D),
        "g_q": gain(ks[6], (L, HEAD_DIM)),
        "g_k": gain(ks[7], (L, HEAD_DIM)),
        "w_sb_out": nrm(ks[8], (L, SB_WIDTH, D), SB_WIDTH),
        "w_pool_group": nrm(ks[9], (L, N_POOL_GROUPS, POOL_GROUP_DIM, POOL_GROUP_DIM), POOL_GROUP_DIM),
        "pool_scale": gain(ks[10], (L, POOL_WIDTH)),
        "w_pool_out": nrm(ks[11], (L, POOL_WIDTH, D), POOL_WIDTH),
        "w_out": nrm(ks[12], (L, D, D), D),
        "g_norm2": gain(ks[13], (L, D)),
        "w_router": nrm(ks[14], (L, D, N_EXPERTS), D),
        "router_bias": 0.01 * jax.random.normal(ks[15], (L, N_EXPERTS), jnp.float32),
        "w_exp_gate": nrm(ks[16], (L, N_EXPERTS, D, EXPERT_DIM), D),
        "w_exp_up": nrm(ks[17], (L, N_EXPERTS, D, EXPERT_DIM), D),
        "w_exp_down": nrm(ks[18], (L, N_EXPERTS, EXPERT_DIM, D), EXPERT_DIM),
        "w_sh_gate": nrm(ks[19], (L, D, SHARED_DIM), D),
        "w_sh_up": nrm(ks[20], (L, D, SHARED_DIM), D),
        "w_sh_down": nrm(ks[21], (L, SHARED_DIM, D), SHARED_DIM),
    }


def reference(x, c, w_ada, b_ada, g_norm1, w_in, g_q, g_k, w_sb_out, w_pool_group,
              pool_scale, w_pool_out, w_out, g_norm2, w_router, router_bias,
              w_exp_gate, w_exp_up, w_exp_down, w_sh_gate, w_sh_up, w_sh_down):
    for l in range(DEPTH):
        x = _layer(x, c, w_ada[l], b_ada[l], g_norm1[l], w_in[l], g_q[l], g_k[l],
                   w_sb_out[l], w_pool_group[l], pool_scale[l], w_pool_out[l],
                   w_out[l], g_norm2[l], w_router[l], router_bias[l],
                   w_exp_gate[l], w_exp_up[l], w_exp_down[l],
                   w_sh_gate[l], w_sh_up[l], w_sh_down[l])
    return x
```

```python
import functools

import jax
import jax.numpy as jnp
from jax import lax
from jax.experimental import pallas as pl
from jax.experimental.pallas import tpu as pltpu

D_MODEL = 1024
SB_HEADS = 8
HEAD_DIM = 64
SB_WIDTH = SB_HEADS * HEAD_DIM
POOL_WINDOWS = (2, 4, 8, 16)
POOL_WIDTH = 512
POOL_GROUP_DIM = POOL_WIDTH // len(POOL_WINDOWS)
MAX_WIN = max(POOL_WINDOWS)
N_EXPERTS = 64
EXPERT_DIM = 128
TOP_K = 8
N_EXPERT_GROUPS = 8
TOPK_GROUPS = 4
SHARED_DIM = 256
ROUTED_SCALE = 2.5
NORM_EPS = 1e-6

LANES = 128
VMEM_LIMIT = 56 * 1024 * 1024

TM_IN = 512
TM_MIX = 512
TM_MOE = 1024
EXPERTS_PER_STEP = 4
ATTN_BLOCK = 256
ATTN_SKIP_LOG = -88.0

F32 = jnp.float32
BF16 = jnp.bfloat16


def _dot(a, b):
    return jnp.dot(a, b, preferred_element_type=F32)


def _split_bf16(x):
    hi = x.astype(BF16)
    lo = (x - hi.astype(F32)).astype(BF16)
    return hi, lo


def _silu(x):
    return x * jax.nn.sigmoid(x)


def _ada_kernel(c_ref, w_ref, b_ref, o_ref):
    c = c_ref[...]
    o_ref[...] = _dot(_silu(c).astype(BF16), w_ref[...].astype(BF16)) + b_ref[...]


def _ada(c_pad, w_ada, b_ada):
    n = w_ada.shape[1]
    tn = D_MODEL
    return pl.pallas_call(
        _ada_kernel,
        out_shape=jax.ShapeDtypeStruct((c_pad.shape[0], n), F32),
        grid=(n // tn,),
        in_specs=[
            pl.BlockSpec(c_pad.shape, lambda i: (0, 0)),
            pl.BlockSpec((D_MODEL, tn), lambda i: (0, i)),
            pl.BlockSpec((1, tn), lambda i: (0, i)),
        ],
        out_specs=pl.BlockSpec((c_pad.shape[0], tn), lambda i: (0, i)),
        compiler_params=pltpu.CompilerParams(dimension_semantics=("arbitrary",),
                                             vmem_limit_bytes=VMEM_LIMIT),
        name="ada",
    )(c_pad, w_ada, b_ada)


def _in_kernel(x_ref, mod_ref, g1_ref, w_ref, gq_ref, gk_ref, bd_ref, wpg_ref, ps_ref, wpo_ref,
               q_ref, k_ref, v_ref, sga_ref, mb_ref, ubuf):
    j = pl.program_id(1)
    tm = x_ref.shape[0]
    x = x_ref[...]
    m = mod_ref[0]
    shift1, scale1 = m[0:1], m[1:2]
    r = lax.rsqrt(jnp.mean(x * x, axis=-1, keepdims=True) + NORM_EPS)
    h = ((x * r) * g1_ref[...]) * (1.0 + scale1) + shift1
    hb = h.astype(BF16)

    def proj(lo, hi):
        return _dot(hb, w_ref[:, lo:hi])

    def headnorm(t, g_ref):
        sq_hi, sq_lo = _split_bf16(t * t)
        ms = _dot(sq_hi, bd_ref[...]) + _dot(sq_lo, bd_ref[...])
        return (t * lax.rsqrt(ms + NORM_EPS)) * g_ref[...]

    q_ref[...] = headnorm(proj(0, SB_WIDTH), gq_ref).astype(BF16)
    k_ref[...] = headnorm(proj(SB_WIDTH, 2 * SB_WIDTH), gk_ref).astype(BF16)
    v_ref[...] = proj(2 * SB_WIDTH, 3 * SB_WIDTH).astype(BF16)

    u0 = 3 * SB_WIDTH
    u = proj(u0, u0 + POOL_WIDTH)

    @pl.when(j == 0)
    def _():
        ubuf[0:MAX_WIN, :] = jnp.zeros((MAX_WIN, POOL_WIDTH), F32)

    @pl.when(j > 0)
    def _():
        ubuf[0:MAX_WIN, :] = ubuf[tm:tm + MAX_WIN, :]

    ubuf[MAX_WIN:MAX_WIN + tm, :] = u
    t = j * tm + lax.broadcasted_iota(jnp.int32, (tm, POOL_GROUP_DIM), 0)
    parts = []
    for gi, w in enumerate(POOL_WINDOWS):
        lo = gi * POOL_GROUP_DIM
        ug = u[:, lo:lo + POOL_GROUP_DIM]
        s = ug
        for i in range(1, w):
            s = s + ubuf[MAX_WIN - i:MAX_WIN - i + tm, lo:lo + POOL_GROUP_DIM]
        cnt = jnp.minimum(t + 1, w).astype(F32)
        mixed = s / cnt - ug
        parts.append(_dot(mixed.astype(BF16), wpg_ref[gi]))
    pooled = jnp.concatenate(parts, axis=1) * ps_ref[...]
    ypool = _dot(pooled.astype(BF16), wpo_ref[...])

    g0 = u0 + POOL_WIDTH
    sga_ref[...] = jax.nn.sigmoid(proj(g0, g0 + D_MODEL)).astype(BF16)
    mb_ref[...] = (jax.nn.sigmoid(proj(g0 + D_MODEL, g0 + 2 * D_MODEL)) * ypool).astype(BF16)


def _inproj(x2, mod3, g1, w_in, gq, gk, bd, wpg, ps, wpo, batch, seq):
    tm = TM_IN
    nj = seq // tm
    tok = x2.shape[0]
    row = lambda b, j: (b * nj + j, 0)
    const2 = lambda b, j: (0, 0)
    out_sd = lambda n: jax.ShapeDtypeStruct((tok, n), BF16)
    return pl.pallas_call(
        _in_kernel,
        out_shape=(out_sd(SB_WIDTH), out_sd(SB_WIDTH), out_sd(SB_WIDTH), out_sd(D_MODEL), out_sd(D_MODEL)),
        grid=(batch, nj),
        in_specs=[
            pl.BlockSpec((tm, D_MODEL), row),
            pl.BlockSpec((1, 6, D_MODEL), lambda b, j: (b, 0, 0)),
            pl.BlockSpec((1, D_MODEL), const2),
            pl.BlockSpec(w_in.shape, const2),
            pl.BlockSpec((1, SB_WIDTH), const2),
            pl.BlockSpec((1, SB_WIDTH), const2),
            pl.BlockSpec((SB_WIDTH, SB_WIDTH), const2),
            pl.BlockSpec(wpg.shape, lambda b, j: (0, 0, 0)),
            pl.BlockSpec((1, POOL_WIDTH), const2),
            pl.BlockSpec((POOL_WIDTH, D_MODEL), const2),
        ],
        out_specs=(
            pl.BlockSpec((tm, SB_WIDTH), row),
            pl.BlockSpec((tm, SB_WIDTH), row),
            pl.BlockSpec((tm, SB_WIDTH), row),
            pl.BlockSpec((tm, D_MODEL), row),
            pl.BlockSpec((tm, D_MODEL), row),
        ),
        scratch_shapes=[pltpu.VMEM((MAX_WIN + tm, POOL_WIDTH), F32)],
        compiler_params=pltpu.CompilerParams(dimension_semantics=("arbitrary", "arbitrary"),
                                             vmem_limit_bytes=VMEM_LIMIT),
        name="inproj",
    )(x2, mod3, g1, w_in, gq, gk, bd, wpg, ps, wpo)


def _attn_kernel(q_ref, k_ref, v_ref, o_ref, tri_ref):
    blk = ATTN_BLOCK
    seq = q_ref.shape[1]
    row = lax.broadcasted_iota(jnp.int32, (blk, blk), 0)
    col = lax.broadcasted_iota(jnp.int32, (blk, blk), 1)
    tri_ref[...] = jnp.where(row > col, 1.0, 0.0).astype(BF16)
    causal = col < row
    lane = lax.broadcasted_iota(jnp.int32, (blk, LANES), 1)

    def block(qh, kb, vb, carry, acc, mask):
        z = lax.dot_general(qh, kb, (((1,), (1,)), ((), ())), preferred_element_type=F32)
        log_fail = jnp.minimum(-z, 0.0) - jnp.log(1.0 + jnp.exp(-jnp.abs(z)))
        log_sig = z + log_fail
        if mask is not None:
            log_fail = jnp.where(mask, log_fail, 0.0)
        lf_hi, lf_lo = _split_bf16(log_fail)
        tri = tri_ref[...]
        between = (_dot(lf_hi, tri) + _dot(lf_lo, tri)) + carry
        a = jnp.exp(log_sig + between)
        if mask is not None:
            a = jnp.where(mask, a, 0.0)
        acc = acc + _dot(a.astype(BF16), vb)
        carry = carry + jnp.sum(log_fail, axis=1, keepdims=True)
        return carry, acc

    def q_body(qi, _):
        qs = pl.multiple_of(qi * blk, blk)
        q2 = q_ref[0, pl.ds(qs, blk), :]
        kd = k_ref[0, pl.ds(qs, blk), :]
        vd = v_ref[0, pl.ds(qs, blk), :]
        outs = []
        for hh in range(2):
            in_head = (lane < HEAD_DIM) if hh == 0 else (lane >= HEAD_DIM)
            qh = jnp.where(in_head, q2, jnp.zeros_like(q2))
            carry0 = jnp.zeros((blk, 1), F32)
            acc0 = jnp.zeros((blk, LANES), F32)
            carry, acc = block(qh, kd, vd, carry0, acc0, causal)

            def cond(st):
                j, carry, _ = st
                return jnp.logical_and(j >= 0, jnp.max(carry) > ATTN_SKIP_LOG)

            def body(st):
                j, carry, acc = st
                ks = pl.multiple_of(j * blk, blk)
                kb = k_ref[0, pl.ds(ks, blk), :]
                vb = v_ref[0, pl.ds(ks, blk), :]
                carry, acc = block(qh, kb, vb, carry, acc, None)
                return j - 1, carry, acc

            _, _, acc = lax.while_loop(cond, body, (qi - 1, carry, acc))
            outs.append(acc)
        o = jnp.where(lane < HEAD_DIM, outs[0], outs[1])
        o_ref[0, pl.ds(qs, blk), :] = o.astype(BF16)
        return 0

    lax.fori_loop(0, seq // blk, q_body, 0)


def _attention(q3, k3, v3):
    batch, seq, width = q3.shape
    spec = pl.BlockSpec((1, seq, LANES), lambda b, p: (b, 0, p))
    return pl.pallas_call(
        _attn_kernel,
        out_shape=jax.ShapeDtypeStruct((batch, seq, width), BF16),
        grid=(batch, width // LANES),
        in_specs=[spec, spec, spec],
        out_specs=spec,
        scratch_shapes=[pltpu.VMEM((ATTN_BLOCK, ATTN_BLOCK), BF16)],
        compiler_params=pltpu.CompilerParams(dimension_semantics=("arbitrary", "arbitrary"),
                                             vmem_limit_bytes=VMEM_LIMIT),
        name="attn",
    )(q3, k3, v3)


def _route(sel, scores):
    per_group = N_EXPERTS // N_EXPERT_GROUPS
    tm = sel.shape[1]
    neg = -jnp.inf
    tiles = [sel[per_group * g:per_group * (g + 1), :] for g in range(N_EXPERT_GROUPS)]
    sc_tiles = [scores[per_group * g:per_group * (g + 1), :] for g in range(N_EXPERT_GROUPS)]

    grp = []
    for tg in tiles:
        m1 = jnp.max(tg, axis=0, keepdims=True)
        eq = tg == m1
        n_eq = jnp.sum(jnp.where(eq, 1.0, 0.0), axis=0, keepdims=True)
        second = jnp.max(jnp.where(eq, neg, tg), axis=0, keepdims=True)
        grp.append(m1 + jnp.where(n_eq >= 2.0, m1, second))

    vals = []
    for g in range(N_EXPERT_GROUPS):
        rank = jnp.zeros((1, tm), F32)
        for gp in range(N_EXPERT_GROUPS):
            if gp == g:
                continue
            beats = (grp[gp] >= grp[g]) if gp < g else (grp[gp] > grp[g])
            rank = rank + jnp.where(beats, 1.0, 0.0)
        vals.append(jnp.where(rank < float(TOPK_GROUPS), tiles[g], neg))

    sub = lax.broadcasted_iota(jnp.int32, (per_group, tm), 0).astype(F32)
    eidx = [sub + float(per_group * g) for g in range(N_EXPERT_GROUPS)]
    chosen = [jnp.zeros((per_group, tm), F32) for _ in range(N_EXPERT_GROUPS)]
    for _ in range(TOP_K):
        m = functools.reduce(jnp.maximum, [jnp.max(v, axis=0, keepdims=True) for v in vals])
        cand = [jnp.min(jnp.where(v == m, e, float(N_EXPERTS)), axis=0, keepdims=True)
                for v, e in zip(vals, eidx)]
        idx = functools.reduce(jnp.minimum, cand)
        for g in range(N_EXPERT_GROUPS):
            pick = eidx[g] == idx
            chosen[g] = jnp.where(pick, 1.0, chosen[g])
            vals[g] = jnp.where(pick, neg, vals[g])

    w = [jnp.where(c > 0.0, s, 0.0) for c, s in zip(chosen, sc_tiles)]
    denom = functools.reduce(jnp.add, [jnp.sum(t, axis=0, keepdims=True) for t in w])
    return jnp.concatenate([t / denom * ROUTED_SCALE for t in w], axis=0)


def _mix_kernel(x_ref, osb_ref, sga_ref, mb_ref, mod_ref, wsb_ref, wout_ref, g2_ref, wrh_ref, wrl_ref,
                rb_ref, wsgu_ref, wsd_ref, h2_ref, base_ref, gates_ref):
    tm = x_ref.shape[0]
    x = x_ref[...]
    m = mod_ref[0]
    gate1, shift2, scale2, gate2 = m[2:3], m[3:4], m[4:5], m[5:6]
    ysb = _dot(osb_ref[...], wsb_ref[...])
    merged = sga_ref[...].astype(F32) * ysb + mb_ref[...].astype(F32)
    x1 = x + gate1 * _dot(merged.astype(BF16), wout_ref[...])

    r = lax.rsqrt(jnp.mean(x1 * x1, axis=-1, keepdims=True) + NORM_EPS)
    h2 = ((x1 * r) * g2_ref[...]) * (1.0 + scale2) + shift2
    h_hi, h_lo = _split_bf16(h2)
    h2_ref[...] = h_hi

    su = _dot(h_hi, wsgu_ref[...])
    sh = _dot((_silu(su[:, :SHARED_DIM]) * su[:, SHARED_DIM:]).astype(BF16), wsd_ref[...])
    base_ref[...] = x1 + gate2 * sh

    logits = _dot(h_hi, wrh_ref[...]) + _dot(h_hi, wrl_ref[...]) + _dot(h_lo, wrh_ref[...])
    scores = jax.nn.sigmoid(logits.T[:N_EXPERTS, :])
    gates = _route(scores + rb_ref[...], scores)
    gates_pad = jnp.concatenate([gates, jnp.zeros((LANES - N_EXPERTS, tm), F32)], axis=0)
    gates_ref[...] = gates_pad.T.astype(BF16)


def _mix(x2, osb, sga, mb, mod3, wsb, wout, g2, wrh, wrl, rb, wsgu, wsd, seq):
    tm = TM_MIX
    tok = x2.shape[0]
    per_batch = seq // tm
    row = lambda i: (i, 0)
    const2 = lambda i: (0, 0)
    return pl.pallas_call(
        _mix_kernel,
        out_shape=(jax.ShapeDtypeStruct((tok, D_MODEL), BF16),
                   jax.ShapeDtypeStruct((tok, D_MODEL), F32),
                   jax.ShapeDtypeStruct((tok, LANES), BF16)),
        grid=(tok // tm,),
        in_specs=[
            pl.BlockSpec((tm, D_MODEL), row),
            pl.BlockSpec((tm, SB_WIDTH), row),
            pl.BlockSpec((tm, D_MODEL), row),
            pl.BlockSpec((tm, D_MODEL), row),
            pl.BlockSpec((1, 6, D_MODEL), lambda i: (i // per_batch, 0, 0)),
            pl.BlockSpec(wsb.shape, const2),
            pl.BlockSpec(wout.shape, const2),
            pl.BlockSpec((1, D_MODEL), const2),
            pl.BlockSpec(wrh.shape, const2),
            pl.BlockSpec(wrl.shape, const2),
            pl.BlockSpec(rb.shape, const2),
            pl.BlockSpec(wsgu.shape, const2),
            pl.BlockSpec(wsd.shape, const2),
        ],
        out_specs=(pl.BlockSpec((tm, D_MODEL), row),
                   pl.BlockSpec((tm, D_MODEL), row),
                   pl.BlockSpec((tm, LANES), row)),
        compiler_params=pltpu.CompilerParams(dimension_semantics=("arbitrary",),
                                             vmem_limit_bytes=VMEM_LIMIT),
        name="mix",
    )(x2, osb, sga, mb, mod3, wsb, wout, g2, wrh, wrl, rb, wsgu, wsd)


def _moe_kernel(h_ref, g_ref, base_ref, mod_ref, wgu_ref, wd_ref, o_ref):
    eg = pl.program_id(1)
    width = EXPERTS_PER_STEP * EXPERT_DIM

    @pl.when(eg == 0)
    def _():
        o_ref[...] = jnp.zeros_like(o_ref)

    up = _dot(h_ref[...], wgu_ref[0])
    a, b = up[:, :width], up[:, width:]
    krow = lax.broadcasted_iota(jnp.int32, (LANES, width), 0)
    kcol = lax.broadcasted_iota(jnp.int32, (LANES, width), 1)
    onehot = jnp.where(krow == eg * EXPERTS_PER_STEP + kcol // EXPERT_DIM, 1.0, 0.0).astype(BF16)
    gexp = _dot(g_ref[...], onehot)
    hmid = (_silu(a) * b * gexp).astype(BF16)
    o_ref[...] += _dot(hmid, wd_ref[0])

    @pl.when(eg == pl.num_programs(1) - 1)
    def _():
        gate2 = mod_ref[0][5:6]
        o_ref[...] = base_ref[...] + gate2 * o_ref[...]


def _moe(h2, gates, base, mod3, wgu, wd, seq):
    tm = TM_MOE
    tok = h2.shape[0]
    per_batch = seq // tm
    n_groups = wgu.shape[0]
    row = lambda i, e: (i, 0)
    return pl.pallas_call(
        _moe_kernel,
        out_shape=jax.ShapeDtypeStruct((tok, D_MODEL), F32),
        grid=(tok // tm, n_groups),
        in_specs=[
            pl.BlockSpec((tm, D_MODEL), row),
            pl.BlockSpec((tm, LANES), row),
            pl.BlockSpec((tm, D_MODEL), row),
            pl.BlockSpec((1, 6, D_MODEL), lambda i, e: (i // per_batch, 0, 0)),
            pl.BlockSpec((1,) + wgu.shape[1:], lambda i, e: (e, 0, 0)),
            pl.BlockSpec((1,) + wd.shape[1:], lambda i, e: (e, 0, 0)),
        ],
        out_specs=pl.BlockSpec((tm, D_MODEL), row),
        compiler_params=pltpu.CompilerParams(dimension_semantics=("arbitrary", "arbitrary"),
                                             vmem_limit_bytes=VMEM_LIMIT),
        name="moe",
    )(h2, gates, base, mod3, wgu, wd)


def _group_expert_weights(w_gate, w_up, w_down):
    e, d, f = w_gate.shape
    g = EXPERTS_PER_STEP

    def regroup(w):
        return w.reshape(e // g, g, d, f).transpose(0, 2, 1, 3).reshape(e // g, d, g * f)

    wgu = jnp.concatenate([regroup(w_gate), regroup(w_up)], axis=2).astype(BF16)
    wd = w_down.reshape(e // g, g * f, d).astype(BF16)
    return wgu, wd


def _layer(x2, c_pad, batch, seq, w_ada, b_ada, g_norm1, w_in, g_q, g_k, w_sb_out, w_pool_group,
           pool_scale, w_pool_out, w_out, g_norm2, w_router, router_bias,
           w_exp_gate, w_exp_up, w_exp_down, w_sh_gate, w_sh_up, w_sh_down):
    mod = _ada(c_pad, w_ada, b_ada[None, :])
    mod3 = mod[:batch].reshape(batch, 6, D_MODEL)

    head_avg = jnp.kron(jnp.eye(SB_HEADS, dtype=F32),
                        jnp.full((HEAD_DIM, HEAD_DIM), 1.0 / HEAD_DIM, F32)).astype(BF16)
    gq = (jnp.tile(g_q, SB_HEADS) * (HEAD_DIM ** -0.5))[None, :]
    gk = jnp.tile(g_k, SB_HEADS)[None, :]
    q, k, v, sga, mb = _inproj(
        x2, mod3, g_norm1[None, :], w_in.astype(BF16), gq, gk, head_avg,
        w_pool_group.astype(BF16), pool_scale[None, :], w_pool_out.astype(BF16), batch, seq)

    shape3 = (batch, seq, SB_WIDTH)
    osb = _attention(q.reshape(shape3), k.reshape(shape3), v.reshape(shape3)).reshape(batch * seq, SB_WIDTH)

    wr = jnp.pad(w_router, ((0, 0), (0, LANES - N_EXPERTS)))
    wr_hi = wr.astype(BF16)
    wr_lo = (wr - wr_hi.astype(F32)).astype(BF16)
    wsgu = jnp.concatenate([w_sh_gate, w_sh_up], axis=1).astype(BF16)
    h2, base, gates = _mix(
        x2, osb, sga, mb, mod3, w_sb_out.astype(BF16), w_out.astype(BF16), g_norm2[None, :],
        wr_hi, wr_lo, router_bias[:, None], wsgu, w_sh_down.astype(BF16), seq)

    wgu, wd = _group_expert_weights(w_exp_gate, w_exp_up, w_exp_down)
    return _moe(h2, gates, base, mod3, wgu, wd, seq)


def kernel(x, c, w_ada, b_ada, g_norm1, w_in, g_q, g_k, w_sb_out, w_pool_group, pool_scale, w_pool_out,
           w_out, g_norm2, w_router, router_bias, w_exp_gate, w_exp_up, w_exp_down,
           w_sh_gate, w_sh_up, w_sh_down):
    batch, seq, d = x.shape
    x2 = x.reshape(batch * seq, d)
    c_pad = jnp.pad(c, ((0, 8 - batch), (0, 0)))
    for l in range(w_ada.shape[0]):
        x2 = _layer(x2, c_pad, batch, seq, w_ada[l], b_ada[l], g_norm1[l], w_in[l], g_q[l], g_k[l],
                    w_sb_out[l], w_pool_group[l], pool_scale[l], w_pool_out[l], w_out[l], g_norm2[l],
                    w_router[l], router_bias[l], w_exp_gate[l], w_exp_up[l], w_exp_down[l],
                    w_sh_gate[l], w_sh_up[l], w_sh_down[l])
    return x2.reshape(batch, seq, d)
```

```python
import functools

import jax
import jax.numpy as jnp
import numpy as np
from jax import lax
from jax.experimental import pallas as pl
from jax.experimental.pallas import tpu as pltpu

D_MODEL = 1024
SB_HEADS = 8
HEAD_DIM = 64
SB_WIDTH = SB_HEADS * HEAD_DIM
POOL_WINDOWS = (2, 4, 8, 16)
POOL_WIDTH = 512
POOL_GROUP_DIM = POOL_WIDTH // len(POOL_WINDOWS)
MAX_WIN = max(POOL_WINDOWS)
N_EXPERTS = 64
EXPERT_DIM = 128
TOP_K = 8
N_EXPERT_GROUPS = 8
TOPK_GROUPS = 4
SHARED_DIM = 256
ROUTED_SCALE = 2.5
NORM_EPS = 1e-6

LANES = 128
MXU_TILE = 256
VMEM_LIMIT = 56 * 1024 * 1024

TM_IN = 512
TM_MIX = 512
TM_MOE = 1024
EXPERTS_PER_STEP = 4
ATTN_Q = 128
ATTN_K = 256
ATTN_SKIP_LOG2 = 127.0
MASKED_LOGIT = -1e30
LOG2_E = 1.4426950408889634

F32 = jnp.float32
BF16 = jnp.bfloat16


def _dot(a, b):
    return jnp.dot(a, b, preferred_element_type=F32)


def _split_bf16(x):
    hi = x.astype(BF16)
    lo = (x - hi.astype(F32)).astype(BF16)
    return hi, lo


def _silu(x):
    return x * jax.nn.sigmoid(x)


def _ada_kernel(c_ref, w_ref, b_ref, o_ref):
    c = c_ref[...]
    o_ref[...] = _dot(_silu(c).astype(BF16), w_ref[...].astype(BF16)) + b_ref[...]


def _ada(c_pad, w_ada, b_ada):
    n = w_ada.shape[1]
    tn = D_MODEL
    return pl.pallas_call(
        _ada_kernel,
        out_shape=jax.ShapeDtypeStruct((c_pad.shape[0], n), F32),
        grid=(n // tn,),
        in_specs=[
            pl.BlockSpec(c_pad.shape, lambda i: (0, 0)),
            pl.BlockSpec((D_MODEL, tn), lambda i: (0, i)),
            pl.BlockSpec((1, tn), lambda i: (0, i)),
        ],
        out_specs=pl.BlockSpec((c_pad.shape[0], tn), lambda i: (0, i)),
        compiler_params=pltpu.CompilerParams(dimension_semantics=("arbitrary",),
                                             vmem_limit_bytes=VMEM_LIMIT),
        name="ada",
    )(c_pad, w_ada, b_ada)


def _in_kernel(x_ref, mod_ref, g1_ref, w_ref, gq_ref, gk_ref, bd_ref, wpg_ref, ps_ref, wpo_ref,
               q_ref, k_ref, v_ref, sga_ref, mb_ref, ubuf):
    j = pl.program_id(1)
    tm = x_ref.shape[0]
    x = x_ref[...]
    m = mod_ref[0]
    shift1, scale1 = m[0:1], m[1:2]
    r = lax.rsqrt(jnp.mean(x * x, axis=-1, keepdims=True) + NORM_EPS)
    h = ((x * r) * g1_ref[...]) * (1.0 + scale1) + shift1
    hb = h.astype(BF16)

    def proj(lo, hi):
        return _dot(hb, w_ref[:, lo:hi])

    def headnorm(t, g_ref):
        sq_hi, sq_lo = _split_bf16(t * t)
        half = bd_ref.shape[1]
        ms = jnp.concatenate(
            [_dot(jnp.concatenate([sq_hi[:, c:c + half], sq_lo[:, c:c + half]], axis=1), bd_ref[...])
             for c in range(0, SB_WIDTH, half)], axis=1)
        return (t * lax.rsqrt(ms + NORM_EPS)) * g_ref[...]

    q_ref[...] = headnorm(proj(0, SB_WIDTH), gq_ref).astype(BF16)
    k_ref[...] = headnorm(proj(SB_WIDTH, 2 * SB_WIDTH), gk_ref).astype(BF16)
    v_ref[...] = proj(2 * SB_WIDTH, 3 * SB_WIDTH).astype(BF16)

    u0 = 3 * SB_WIDTH
    u = proj(u0, u0 + POOL_WIDTH)

    @pl.when(j == 0)
    def _():
        ubuf[0:MAX_WIN, :] = jnp.zeros((MAX_WIN, POOL_WIDTH), F32)

    @pl.when(j > 0)
    def _():
        ubuf[0:MAX_WIN, :] = ubuf[tm:tm + MAX_WIN, :]

    ubuf[MAX_WIN:MAX_WIN + tm, :] = u
    t = j * tm + lax.broadcasted_iota(jnp.int32, (tm, POOL_GROUP_DIM), 0)
    mixed = []
    for gi, w in enumerate(POOL_WINDOWS):
        lo = gi * POOL_GROUP_DIM
        ug = u[:, lo:lo + POOL_GROUP_DIM]
        s = ug
        for i in range(1, w):
            s = s + ubuf[MAX_WIN - i:MAX_WIN - i + tm, lo:lo + POOL_GROUP_DIM]
        cnt = jnp.minimum(t + 1, w).astype(F32)
        mixed.append((s / cnt - ug).astype(BF16))
    parts = [_dot(jnp.concatenate(mixed[2 * i:2 * i + 2], axis=1), wpg_ref[i])
             for i in range(len(POOL_WINDOWS) // 2)]
    pooled = jnp.concatenate(parts, axis=1) * ps_ref[...]
    ypool = _dot(pooled.astype(BF16), wpo_ref[...])

    g0 = u0 + POOL_WIDTH
    sga_ref[...] = jax.nn.sigmoid(proj(g0, g0 + D_MODEL)).astype(BF16)
    mb_ref[...] = (jax.nn.sigmoid(proj(g0 + D_MODEL, g0 + 2 * D_MODEL)) * ypool).astype(BF16)


def _inproj(x2, mod3, g1, w_in, gq, gk, bd, wpg, ps, wpo, batch, seq):
    tm = TM_IN
    nj = seq // tm
    tok = x2.shape[0]
    row = lambda b, j: (b * nj + j, 0)
    const2 = lambda b, j: (0, 0)
    out_sd = lambda n: jax.ShapeDtypeStruct((tok, n), BF16)
    return pl.pallas_call(
        _in_kernel,
        out_shape=(out_sd(SB_WIDTH), out_sd(SB_WIDTH), out_sd(SB_WIDTH), out_sd(D_MODEL), out_sd(D_MODEL)),
        grid=(batch, nj),
        in_specs=[
            pl.BlockSpec((tm, D_MODEL), row),
            pl.BlockSpec((1, 6, D_MODEL), lambda b, j: (b, 0, 0)),
            pl.BlockSpec((1, D_MODEL), const2),
            pl.BlockSpec(w_in.shape, const2),
            pl.BlockSpec((1, SB_WIDTH), const2),
            pl.BlockSpec((1, SB_WIDTH), const2),
            pl.BlockSpec(bd.shape, const2),
            pl.BlockSpec(wpg.shape, lambda b, j: (0, 0, 0)),
            pl.BlockSpec((1, POOL_WIDTH), const2),
            pl.BlockSpec((POOL_WIDTH, D_MODEL), const2),
        ],
        out_specs=(
            pl.BlockSpec((tm, SB_WIDTH), row),
            pl.BlockSpec((tm, SB_WIDTH), row),
            pl.BlockSpec((tm, SB_WIDTH), row),
            pl.BlockSpec((tm, D_MODEL), row),
            pl.BlockSpec((tm, D_MODEL), row),
        ),
        scratch_shapes=[pltpu.VMEM((MAX_WIN + tm, POOL_WIDTH), F32)],
        compiler_params=pltpu.CompilerParams(dimension_semantics=("arbitrary", "arbitrary"),
                                             vmem_limit_bytes=VMEM_LIMIT),
        name="inproj",
    )(x2, mod3, g1, w_in, gq, gk, bd, wpg, ps, wpo)


def _attn_kernel(q_ref, k_ref, v_ref, o_ref, tri_ref):
    tq, tk = ATTN_Q, ATTN_K
    seq = q_ref.shape[1]
    n_pairs = q_ref.shape[2] // LANES
    row = lax.broadcasted_iota(jnp.int32, (tk, tk), 0)
    col = lax.broadcasted_iota(jnp.int32, (tk, tk), 1)
    tri = jnp.where(row > col, 1.0, 0.0).astype(BF16)
    tri_ref[0:tk, :] = tri
    tri_ref[tk:2 * tk, :] = tri
    kcol = lax.broadcasted_iota(jnp.int32, (tq, tk), 1)
    col_minus_row = kcol - lax.broadcasted_iota(jnp.int32, (tq, tk), 0)
    first_head = lax.broadcasted_iota(jnp.int32, (tq, LANES), 1) < HEAD_DIM

    n_heads = 2 * n_pairs

    def window(q_pairs, start, carry, accs, mask):
        zs = []
        for p in range(n_pairs):
            kb = k_ref[0, pl.ds(start, tk), p * LANES:(p + 1) * LANES]
            zs.append(lax.dot_general(q_pairs[p], kb, (((1,), (1,)), ((), ())), preferred_element_type=F32))
        z = jnp.concatenate(zs, axis=0).reshape(n_heads, tq, tk)
        z = jnp.where(mask[None], z, MASKED_LOGIT)
        fail = jnp.maximum(z, 0.0) + jnp.log2(1.0 + jnp.exp2(-jnp.abs(z)))
        f_hi, f_lo = _split_bf16(fail)
        hi_lo = jnp.concatenate([f_hi, f_lo], axis=-1).reshape(n_heads * tq, 2 * tk)
        between = _dot(hi_lo, tri_ref[...]).reshape(n_heads, tq, tk)
        a = jnp.exp2(((z - fail) - between) - carry).astype(BF16)
        new_accs = []
        for p in range(n_pairs):
            vb = v_ref[0, pl.ds(start, tk), p * LANES:(p + 1) * LANES]
            av = _dot(a[2 * p:2 * p + 2].reshape(2 * tq, tk), vb)
            new_accs.append(accs[2 * p] + av[:tq])
            new_accs.append(accs[2 * p + 1] + av[tq:])
        carry = carry + jnp.sum(fail, axis=-1, keepdims=True)
        return carry, tuple(new_accs)

    def q_body(qi, _):
        qs = pl.multiple_of(qi * tq, tq)
        ws = pl.multiple_of(jnp.maximum(qs - (tk - tq), 0), tq)
        causal = col_minus_row < (qs - ws)
        q_pairs = []
        for p in range(n_pairs):
            q2 = q_ref[0, pl.ds(qs, tq), p * LANES:(p + 1) * LANES]
            zero = jnp.zeros_like(q2)
            q_pairs.append(jnp.concatenate([jnp.where(first_head, q2, zero),
                                            jnp.where(first_head, zero, q2)], axis=0))
        carry0 = jnp.zeros((n_heads, tq, 1), F32)
        accs0 = tuple(jnp.zeros((tq, LANES), F32) for _ in range(n_heads))
        carry, accs = window(q_pairs, ws, carry0, accs0, causal)

        def cond(st):
            start, carry, _ = st
            return jnp.logical_and(start > 0, jnp.min(carry) < ATTN_SKIP_LOG2)

        def body(st):
            start, carry, accs = st
            nxt = pl.multiple_of(jnp.maximum(start - tk, 0), tq)
            carry, accs = window(q_pairs, nxt, carry, accs, kcol < (start - nxt))
            return nxt, carry, accs

        _, _, accs = lax.while_loop(cond, body, (ws, carry, accs))
        o = jnp.concatenate([jnp.where(first_head, accs[2 * p], accs[2 * p + 1]) for p in range(n_pairs)],
                            axis=1)
        o_ref[0, pl.ds(qs, tq), :] = o.astype(BF16)
        return 0

    lax.fori_loop(0, seq // tq, q_body, 0)


def _attention(q3, k3, v3):
    batch, seq, width = q3.shape
    spec = pl.BlockSpec((1, seq, width), lambda b: (b, 0, 0))
    return pl.pallas_call(
        _attn_kernel,
        out_shape=jax.ShapeDtypeStruct((batch, seq, width), BF16),
        grid=(batch,),
        in_specs=[spec, spec, spec],
        out_specs=spec,
        scratch_shapes=[pltpu.VMEM((2 * ATTN_K, ATTN_K), BF16)],
        compiler_params=pltpu.CompilerParams(dimension_semantics=("arbitrary",),
                                             vmem_limit_bytes=VMEM_LIMIT),
        name="attn",
    )(q3, k3, v3)


def _route(sel, scores):
    per_group = N_EXPERTS // N_EXPERT_GROUPS
    tm = sel.shape[1]
    neg = -jnp.inf
    tiles = [sel[per_group * g:per_group * (g + 1), :] for g in range(N_EXPERT_GROUPS)]
    sc_tiles = [scores[per_group * g:per_group * (g + 1), :] for g in range(N_EXPERT_GROUPS)]

    grp = []
    for tg in tiles:
        m1 = jnp.max(tg, axis=0, keepdims=True)
        eq = tg == m1
        n_eq = jnp.sum(jnp.where(eq, 1.0, 0.0), axis=0, keepdims=True)
        second = jnp.max(jnp.where(eq, neg, tg), axis=0, keepdims=True)
        grp.append(m1 + jnp.where(n_eq >= 2.0, m1, second))

    vals = []
    for g in range(N_EXPERT_GROUPS):
        rank = jnp.zeros((1, tm), F32)
        for gp in range(N_EXPERT_GROUPS):
            if gp == g:
                continue
            beats = (grp[gp] >= grp[g]) if gp < g else (grp[gp] > grp[g])
            rank = rank + jnp.where(beats, 1.0, 0.0)
        vals.append(jnp.where(rank < float(TOPK_GROUPS), tiles[g], neg))

    sub = lax.broadcasted_iota(jnp.int32, (per_group, tm), 0).astype(F32)
    eidx = [sub + float(per_group * g) for g in range(N_EXPERT_GROUPS)]
    chosen = [jnp.zeros((per_group, tm), F32) for _ in range(N_EXPERT_GROUPS)]
    for _ in range(TOP_K):
        m = functools.reduce(jnp.maximum, [jnp.max(v, axis=0, keepdims=True) for v in vals])
        cand = [jnp.min(jnp.where(v == m, e, float(N_EXPERTS)), axis=0, keepdims=True)
                for v, e in zip(vals, eidx)]
        idx = functools.reduce(jnp.minimum, cand)
        for g in range(N_EXPERT_GROUPS):
            pick = eidx[g] == idx
            chosen[g] = jnp.where(pick, 1.0, chosen[g])
            vals[g] = jnp.where(pick, neg, vals[g])

    w = [jnp.where(c > 0.0, s, 0.0) for c, s in zip(chosen, sc_tiles)]
    denom = functools.reduce(jnp.add, [jnp.sum(t, axis=0, keepdims=True) for t in w])
    return jnp.concatenate([t / denom * ROUTED_SCALE for t in w], axis=0)


def _mix_kernel(x_ref, osb_ref, sga_ref, mb_ref, mod_ref, wsb_ref, wout_ref, g2_ref, wrh_ref, wrl_ref,
                rb_ref, wsgu_ref, wsd_ref, h2_ref, base_ref, gates_ref):
    tm = x_ref.shape[0]
    x = x_ref[...]
    m = mod_ref[0]
    gate1, shift2, scale2, gate2 = m[2:3], m[3:4], m[4:5], m[5:6]
    ysb = _dot(osb_ref[...], wsb_ref[...])
    merged = sga_ref[...].astype(F32) * ysb + mb_ref[...].astype(F32)
    x1 = x + gate1 * _dot(merged.astype(BF16), wout_ref[...])

    r = lax.rsqrt(jnp.mean(x1 * x1, axis=-1, keepdims=True) + NORM_EPS)
    h2 = ((x1 * r) * g2_ref[...]) * (1.0 + scale2) + shift2
    h_hi, h_lo = _split_bf16(h2)
    h2_ref[...] = h_hi

    su = _dot(h_hi, wsgu_ref[...])
    sh = _dot((_silu(su[:, :SHARED_DIM]) * su[:, SHARED_DIM:]).astype(BF16), wsd_ref[...])
    base_ref[...] = x1 + gate2 * sh

    logits = _dot(h_hi, wrh_ref[...]) + _dot(h_hi, wrl_ref[...]) + _dot(h_lo, wrh_ref[...])
    scores = jax.nn.sigmoid(logits.T[:N_EXPERTS, :])
    gates = _route(scores + rb_ref[...], scores)
    gates_pad = jnp.concatenate([gates, jnp.zeros((LANES - N_EXPERTS, tm), F32)], axis=0)
    gates_ref[...] = gates_pad.T.astype(BF16)


def _mix(x2, osb, sga, mb, mod3, wsb, wout, g2, wrh, wrl, rb, wsgu, wsd, seq):
    tm = TM_MIX
    tok = x2.shape[0]
    per_batch = seq // tm
    row = lambda i: (i, 0)
    const2 = lambda i: (0, 0)
    return pl.pallas_call(
        _mix_kernel,
        out_shape=(jax.ShapeDtypeStruct((tok, D_MODEL), BF16),
                   jax.ShapeDtypeStruct((tok, D_MODEL), F32),
                   jax.ShapeDtypeStruct((tok, LANES), BF16)),
        grid=(tok // tm,),
        in_specs=[
            pl.BlockSpec((tm, D_MODEL), row),
            pl.BlockSpec((tm, SB_WIDTH), row),
            pl.BlockSpec((tm, D_MODEL), row),
            pl.BlockSpec((tm, D_MODEL), row),
            pl.BlockSpec((1, 6, D_MODEL), lambda i: (i // per_batch, 0, 0)),
            pl.BlockSpec(wsb.shape, const2),
            pl.BlockSpec(wout.shape, const2),
            pl.BlockSpec((1, D_MODEL), const2),
            pl.BlockSpec(wrh.shape, const2),
            pl.BlockSpec(wrl.shape, const2),
            pl.BlockSpec(rb.shape, const2),
            pl.BlockSpec(wsgu.shape, const2),
            pl.BlockSpec(wsd.shape, const2),
        ],
        out_specs=(pl.BlockSpec((tm, D_MODEL), row),
                   pl.BlockSpec((tm, D_MODEL), row),
                   pl.BlockSpec((tm, LANES), row)),
        compiler_params=pltpu.CompilerParams(dimension_semantics=("arbitrary",),
                                             vmem_limit_bytes=VMEM_LIMIT),
        name="mix",
    )(x2, osb, sga, mb, mod3, wsb, wout, g2, wrh, wrl, rb, wsgu, wsd)


def _moe_kernel(h_ref, g_ref, base_ref, mod_ref, wg_ref, wu_ref, wd_ref, o_ref):
    eg = pl.program_id(1)
    width = EXPERTS_PER_STEP * EXPERT_DIM

    @pl.when(eg == 0)
    def _():
        o_ref[...] = jnp.zeros_like(o_ref)

    wgu = jnp.concatenate([wg_ref[j] for j in range(EXPERTS_PER_STEP)]
                          + [wu_ref[j] for j in range(EXPERTS_PER_STEP)], axis=1)
    up = _dot(h_ref[...], wgu)
    a, b = up[:, :width], up[:, width:]
    krow = lax.broadcasted_iota(jnp.int32, (LANES, width), 0)
    kcol = lax.broadcasted_iota(jnp.int32, (LANES, width), 1)
    onehot = jnp.where(krow == eg * EXPERTS_PER_STEP + kcol // EXPERT_DIM, 1.0, 0.0).astype(BF16)
    gexp = _dot(g_ref[...], onehot)
    hmid = (_silu(a) * b * gexp).astype(BF16)
    wd = jnp.concatenate([wd_ref[j] for j in range(EXPERTS_PER_STEP)], axis=0)
    o_ref[...] += _dot(hmid, wd)

    @pl.when(eg == pl.num_programs(1) - 1)
    def _():
        gate2 = mod_ref[0][5:6]
        o_ref[...] = base_ref[...] + gate2 * o_ref[...]


def _moe(h2, gates, base, mod3, wg, wu, wd, seq):
    tm = TM_MOE
    tok = h2.shape[0]
    per_batch = seq // tm
    g = EXPERTS_PER_STEP
    row = lambda i, e: (i, 0)
    experts = lambda i, e: (e, 0, 0)
    return pl.pallas_call(
        _moe_kernel,
        out_shape=jax.ShapeDtypeStruct((tok, D_MODEL), F32),
        grid=(tok // tm, N_EXPERTS // g),
        in_specs=[
            pl.BlockSpec((tm, D_MODEL), row),
            pl.BlockSpec((tm, LANES), row),
            pl.BlockSpec((tm, D_MODEL), row),
            pl.BlockSpec((1, 6, D_MODEL), lambda i, e: (i // per_batch, 0, 0)),
            pl.BlockSpec((g, D_MODEL, EXPERT_DIM), experts),
            pl.BlockSpec((g, D_MODEL, EXPERT_DIM), experts),
            pl.BlockSpec((g, EXPERT_DIM, D_MODEL), experts),
        ],
        out_specs=pl.BlockSpec((tm, D_MODEL), row),
        compiler_params=pltpu.CompilerParams(dimension_semantics=("arbitrary", "arbitrary"),
                                             vmem_limit_bytes=VMEM_LIMIT),
        name="moe",
    )(h2, gates, base, mod3, wg, wu, wd)


def _block_diag(blocks):
    n = len(blocks)
    rows = [jnp.concatenate([blocks[i] if i == j else jnp.zeros_like(blocks[j]) for j in range(n)], axis=1)
            for i in range(n)]
    return jnp.concatenate(rows, axis=0)


def _layer(x2, c_pad, batch, seq, w_ada, b_ada, g_norm1, w_in, g_q, g_k, w_sb_out, w_pool_group,
           pool_scale, w_pool_out, w_out, g_norm2, w_router, router_bias,
           w_exp_gate, w_exp_up, w_exp_down, w_sh_gate, w_sh_up, w_sh_down):
    mod = _ada(c_pad, w_ada, b_ada[None, :])
    mod3 = mod[:batch].reshape(batch, 6, D_MODEL)

    heads_per_tile = MXU_TILE // HEAD_DIM
    head_avg = np.kron(np.eye(heads_per_tile, dtype=np.float32),
                       np.full((HEAD_DIM, HEAD_DIM), 1.0 / HEAD_DIM, np.float32))
    head_avg2 = jnp.asarray(np.concatenate([head_avg, head_avg], axis=0), dtype=BF16)
    gq = (jnp.tile(g_q, SB_HEADS) * (HEAD_DIM ** -0.5 * LOG2_E))[None, :]
    gk = jnp.tile(g_k, SB_HEADS)[None, :]
    wpg = w_pool_group.astype(BF16)
    wpg_pairs = jnp.stack([_block_diag([wpg[2 * i], wpg[2 * i + 1]])
                           for i in range(len(POOL_WINDOWS) // 2)])
    q, k, v, sga, mb = _inproj(
        x2, mod3, g_norm1[None, :], w_in.astype(BF16), gq, gk, head_avg2,
        wpg_pairs, pool_scale[None, :], w_pool_out.astype(BF16), batch, seq)

    shape3 = (batch, seq, SB_WIDTH)
    osb = _attention(q.reshape(shape3), k.reshape(shape3), v.reshape(shape3)).reshape(batch * seq, SB_WIDTH)

    wr = jnp.pad(w_router, ((0, 0), (0, LANES - N_EXPERTS)))
    wr_hi = wr.astype(BF16)
    wr_lo = (wr - wr_hi.astype(F32)).astype(BF16)
    wsgu = jnp.concatenate([w_sh_gate, w_sh_up], axis=1).astype(BF16)
    h2, base, gates = _mix(
        x2, osb, sga, mb, mod3, w_sb_out.astype(BF16), w_out.astype(BF16), g_norm2[None, :],
        wr_hi, wr_lo, router_bias[:, None], wsgu, w_sh_down.astype(BF16), seq)

    return _moe(h2, gates, base, mod3, w_exp_gate.astype(BF16), w_exp_up.astype(BF16),
                w_exp_down.astype(BF16), seq)


def kernel(x, c, w_ada, b_ada, g_norm1, w_in, g_q, g_k, w_sb_out, w_pool_group, pool_scale, w_pool_out,
           w_out, g_norm2, w_router, router_bias, w_exp_gate, w_exp_up, w_exp_down,
           w_sh_gate, w_sh_up, w_sh_down):
    batch, seq, d = x.shape
    x2 = x.reshape(batch * seq, d)
    c_pad = jnp.pad(c, ((0, 8 - batch), (0, 0)))
    for l in range(w_ada.shape[0]):
        x2 = _layer(x2, c_pad, batch, seq, w_ada[l], b_ada[l], g_norm1[l], w_in[l], g_q[l], g_k[l],
                    w_sb_out[l], w_pool_group[l], pool_scale[l], w_pool_out[l], w_out[l], g_norm2[l],
                    w_router[l], router_bias[l], w_exp_gate[l], w_exp_up[l], w_exp_down[l],
                    w_sh_gate[l], w_sh_up[l], w_sh_down[l])
    return x2.reshape(batch, seq, d)
```

```python
import functools

import jax
import jax.numpy as jnp
import numpy as np
from jax import lax
from jax.experimental import pallas as pl
from jax.experimental.pallas import tpu as pltpu

D_MODEL = 1024
SB_HEADS = 8
HEAD_DIM = 64
SB_WIDTH = SB_HEADS * HEAD_DIM
POOL_WINDOWS = (2, 4, 8, 16)
POOL_WIDTH = 512
POOL_GROUP_DIM = POOL_WIDTH // len(POOL_WINDOWS)
MAX_WIN = max(POOL_WINDOWS)
N_EXPERTS = 64
EXPERT_DIM = 128
TOP_K = 8
N_EXPERT_GROUPS = 8
TOPK_GROUPS = 4
SHARED_DIM = 256
ROUTED_SCALE = 2.5
NORM_EPS = 1e-6

LANES = 128
MXU_TILE = 256
VMEM_LIMIT = 56 * 1024 * 1024

TM_IN = 512
TM_MIX = 512
MIX_ROW_SPLITS = 2
IN_ROW_SPLITS = 2
TM_MOE = 2048
MOE_ROW_SPLITS = 4
EXPERTS_PER_STEP = 4
ATTN_Q = 128
ATTN_UNROLL = 2
ATTN_K = 384
ATTN_SKIP_LOG2 = 127.0
MASKED_LOGIT = -1e30
LOG2_E = 1.4426950408889634

F32 = jnp.float32
BF16 = jnp.bfloat16


def _dot(a, b):
    return jnp.dot(a, b, preferred_element_type=F32)


def _split_bf16(x):
    hi = x.astype(BF16)
    lo = (x - hi.astype(F32)).astype(BF16)
    return hi, lo


def _silu(x):
    return x * jax.nn.sigmoid(x)


def _ada_kernel(c_ref, w_ref, b_ref, o_ref):
    c = c_ref[...]
    o_ref[...] = _dot(_silu(c).astype(BF16), w_ref[...].astype(BF16)) + b_ref[...]


def _ada(c_pad, w_ada, b_ada):
    n = w_ada.shape[1]
    tn = D_MODEL
    return pl.pallas_call(
        _ada_kernel,
        out_shape=jax.ShapeDtypeStruct((c_pad.shape[0], n), F32),
        grid=(n // tn,),
        in_specs=[
            pl.BlockSpec(c_pad.shape, lambda i: (0, 0)),
            pl.BlockSpec((D_MODEL, tn), lambda i: (0, i)),
            pl.BlockSpec((1, tn), lambda i: (0, i)),
        ],
        out_specs=pl.BlockSpec((c_pad.shape[0], tn), lambda i: (0, i)),
        compiler_params=pltpu.CompilerParams(dimension_semantics=("arbitrary",),
                                             vmem_limit_bytes=VMEM_LIMIT),
        name="ada",
    )(c_pad, w_ada, b_ada)


def _in_kernel(x_ref, mod_ref, g1_ref, w_ref, gq_ref, gk_ref, bd_ref, wpg_ref, ps_ref, wpo_ref,
               q_ref, k_ref, v_ref, sga_ref, mb_ref, ubuf):
    j = pl.program_id(1)
    tm = x_ref.shape[0]
    m = mod_ref[0]
    shift1, scale1 = m[0:1], m[1:2]

    def headnorm(t, g_ref):
        sq_hi, sq_lo = _split_bf16(t * t)
        half = bd_ref.shape[1]
        ms = jnp.concatenate(
            [_dot(jnp.concatenate([sq_hi[:, c:c + half], sq_lo[:, c:c + half]], axis=1), bd_ref[...])
             for c in range(0, SB_WIDTH, half)], axis=1)
        return (t * lax.rsqrt(ms + NORM_EPS)) * g_ref[...]

    @pl.when(j == 0)
    def _():
        ubuf[0:MAX_WIN, :] = jnp.zeros((MAX_WIN, POOL_WIDTH), F32)

    @pl.when(j > 0)
    def _():
        ubuf[0:MAX_WIN, :] = ubuf[tm:tm + MAX_WIN, :]

    rows = tm // IN_ROW_SPLITS
    u0 = 3 * SB_WIDTH
    g0 = u0 + POOL_WIDTH
    for c in range(IN_ROW_SPLITS):
        r0 = c * rows
        rs = slice(r0, r0 + rows)
        x = x_ref[rs, :]
        r = lax.rsqrt(jnp.mean(x * x, axis=-1, keepdims=True) + NORM_EPS)
        hb = (((x * r) * g1_ref[...]) * (1.0 + scale1) + shift1).astype(BF16)

        def proj(lo, hi):
            return _dot(hb, w_ref[:, lo:hi])

        q_ref[rs, :] = headnorm(proj(0, SB_WIDTH), gq_ref).astype(BF16)
        k_ref[rs, :] = headnorm(proj(SB_WIDTH, 2 * SB_WIDTH), gk_ref).astype(BF16)
        v_ref[rs, :] = proj(2 * SB_WIDTH, 3 * SB_WIDTH).astype(BF16)

        u = proj(u0, u0 + POOL_WIDTH)
        ubuf[MAX_WIN + r0:MAX_WIN + r0 + rows, :] = u
        t = j * tm + r0 + lax.broadcasted_iota(jnp.int32, (rows, POOL_GROUP_DIM), 0)
        mixed = []
        for gi, w in enumerate(POOL_WINDOWS):
            lo = gi * POOL_GROUP_DIM
            ug = u[:, lo:lo + POOL_GROUP_DIM]
            s = ug
            for i in range(1, w):
                s = s + ubuf[MAX_WIN + r0 - i:MAX_WIN + r0 - i + rows, lo:lo + POOL_GROUP_DIM]
            cnt = jnp.minimum(t + 1, w).astype(F32)
            mixed.append((s / cnt - ug).astype(BF16))
        parts = [_dot(jnp.concatenate(mixed[2 * i:2 * i + 2], axis=1), wpg_ref[i])
                 for i in range(len(POOL_WINDOWS) // 2)]
        pooled = jnp.concatenate(parts, axis=1) * ps_ref[...]
        ypool = _dot(pooled.astype(BF16), wpo_ref[...])

        sga_ref[rs, :] = jax.nn.sigmoid(proj(g0, g0 + D_MODEL)).astype(BF16)
        mb_ref[rs, :] = (jax.nn.sigmoid(proj(g0 + D_MODEL, g0 + 2 * D_MODEL)) * ypool).astype(BF16)


def _inproj(x2, mod3, g1, w_in, gq, gk, bd, wpg, ps, wpo, batch, seq):
    tm = TM_IN
    nj = seq // tm
    tok = x2.shape[0]
    row = lambda b, j: (b * nj + j, 0)
    const2 = lambda b, j: (0, 0)
    out_sd = lambda n: jax.ShapeDtypeStruct((tok, n), BF16)
    return pl.pallas_call(
        _in_kernel,
        out_shape=(out_sd(SB_WIDTH), out_sd(SB_WIDTH), out_sd(SB_WIDTH), out_sd(D_MODEL), out_sd(D_MODEL)),
        grid=(batch, nj),
        in_specs=[
            pl.BlockSpec((tm, D_MODEL), row),
            pl.BlockSpec((1, 6, D_MODEL), lambda b, j: (b, 0, 0)),
            pl.BlockSpec((1, D_MODEL), const2),
            pl.BlockSpec(w_in.shape, const2),
            pl.BlockSpec((1, SB_WIDTH), const2),
            pl.BlockSpec((1, SB_WIDTH), const2),
            pl.BlockSpec(bd.shape, const2),
            pl.BlockSpec(wpg.shape, lambda b, j: (0, 0, 0)),
            pl.BlockSpec((1, POOL_WIDTH), const2),
            pl.BlockSpec((POOL_WIDTH, D_MODEL), const2),
        ],
        out_specs=(
            pl.BlockSpec((tm, SB_WIDTH), row),
            pl.BlockSpec((tm, SB_WIDTH), row),
            pl.BlockSpec((tm, SB_WIDTH), row),
            pl.BlockSpec((tm, D_MODEL), row),
            pl.BlockSpec((tm, D_MODEL), row),
        ),
        scratch_shapes=[pltpu.VMEM((MAX_WIN + tm, POOL_WIDTH), F32)],
        compiler_params=pltpu.CompilerParams(dimension_semantics=("arbitrary", "arbitrary"),
                                             vmem_limit_bytes=VMEM_LIMIT),
        name="inproj",
    )(x2, mod3, g1, w_in, gq, gk, bd, wpg, ps, wpo)


def _attn_kernel(q_ref, k_ref, v_ref, o_ref, tri_ref):
    tq, tk = ATTN_Q, ATTN_K
    seq = q_ref.shape[1]
    n_pairs = q_ref.shape[2] // LANES
    row = lax.broadcasted_iota(jnp.int32, (tk, tk), 0)
    col = lax.broadcasted_iota(jnp.int32, (tk, tk), 1)
    tri_ref[...] = jnp.where(row > col, 1.0, 0.0).astype(BF16)
    kcol = lax.broadcasted_iota(jnp.int32, (tq, tk), 1)
    col_minus_row = kcol - lax.broadcasted_iota(jnp.int32, (tq, tk), 0)
    first_head = lax.broadcasted_iota(jnp.int32, (tq, LANES), 1) < HEAD_DIM

    n_heads = 2 * n_pairs

    def window(q_pairs, start, carry, accs, mask):
        zs = []
        for p in range(n_pairs):
            kb = k_ref[0, pl.ds(start, tk), p * LANES:(p + 1) * LANES]
            zs.append(lax.dot_general(q_pairs[p], kb, (((1,), (1,)), ((), ())), preferred_element_type=F32))
        z = jnp.concatenate(zs, axis=0).reshape(n_heads, tq, tk)
        z = jnp.where(mask[None], z, MASKED_LOGIT)
        fail = jnp.maximum(z, 0.0) + jnp.log2(1.0 + jnp.exp2(-jnp.abs(z)))
        between = _dot(fail.astype(BF16).reshape(n_heads * tq, tk), tri_ref[...]).reshape(n_heads, tq, tk)
        a = jnp.exp2(((z - fail) - between) - carry).astype(BF16)
        new_accs = []
        for p in range(n_pairs):
            vb = v_ref[0, pl.ds(start, tk), p * LANES:(p + 1) * LANES]
            av = _dot(a[2 * p:2 * p + 2].reshape(2 * tq, tk), vb)
            new_accs.append(accs[2 * p] + av[:tq])
            new_accs.append(accs[2 * p + 1] + av[tq:])
        carry = carry + jnp.sum(fail, axis=-1, keepdims=True)
        return carry, tuple(new_accs)

    def q_body(it, _):
        blocks = []
        for u in range(ATTN_UNROLL):
            qs = pl.multiple_of((it * ATTN_UNROLL + u) * tq, tq)
            ws = pl.multiple_of(jnp.maximum(qs - (tk - tq), 0), tq)
            causal = col_minus_row < (qs - ws)
            q_pairs = []
            for p in range(n_pairs):
                q2 = q_ref[0, pl.ds(qs, tq), p * LANES:(p + 1) * LANES]
                zero = jnp.zeros_like(q2)
                q_pairs.append(jnp.concatenate([jnp.where(first_head, q2, zero),
                                                jnp.where(first_head, zero, q2)], axis=0))
            carry0 = jnp.zeros((n_heads, tq, 1), F32)
            accs0 = tuple(jnp.zeros((tq, LANES), F32) for _ in range(n_heads))
            carry, accs = window(q_pairs, ws, carry0, accs0, causal)
            blocks.append((qs, ws, q_pairs, carry, accs))

        for qs, ws, q_pairs, carry, accs in blocks:
            def cond(st):
                start, carry, _ = st
                return jnp.logical_and(start > 0, jnp.min(carry) < ATTN_SKIP_LOG2)

            def body(st, q_pairs=q_pairs):
                start, carry, accs = st
                nxt = pl.multiple_of(jnp.maximum(start - tk, 0), tq)
                carry, accs = window(q_pairs, nxt, carry, accs, kcol < (start - nxt))
                return nxt, carry, accs

            _, _, accs = lax.while_loop(cond, body, (ws, carry, accs))
            o = jnp.concatenate([jnp.where(first_head, accs[2 * p], accs[2 * p + 1])
                                 for p in range(n_pairs)], axis=1)
            o_ref[0, pl.ds(qs, tq), :] = o.astype(BF16)
        return 0

    lax.fori_loop(0, seq // (tq * ATTN_UNROLL), q_body, 0)


def _attention(q3, k3, v3):
    batch, seq, width = q3.shape
    spec = pl.BlockSpec((1, seq, width), lambda b: (b, 0, 0))
    return pl.pallas_call(
        _attn_kernel,
        out_shape=jax.ShapeDtypeStruct((batch, seq, width), BF16),
        grid=(batch,),
        in_specs=[spec, spec, spec],
        out_specs=spec,
        scratch_shapes=[pltpu.VMEM((ATTN_K, ATTN_K), BF16)],
        compiler_params=pltpu.CompilerParams(dimension_semantics=("arbitrary",),
                                             vmem_limit_bytes=VMEM_LIMIT),
        name="attn",
    )(q3, k3, v3)


def _route(sel, scores):
    per_group = N_EXPERTS // N_EXPERT_GROUPS
    tm = sel.shape[1]
    neg = -jnp.inf
    tiles = [sel[per_group * g:per_group * (g + 1), :] for g in range(N_EXPERT_GROUPS)]
    sc_tiles = [scores[per_group * g:per_group * (g + 1), :] for g in range(N_EXPERT_GROUPS)]

    grp = []
    for tg in tiles:
        m1 = jnp.max(tg, axis=0, keepdims=True)
        eq = tg == m1
        n_eq = jnp.sum(jnp.where(eq, 1.0, 0.0), axis=0, keepdims=True)
        second = jnp.max(jnp.where(eq, neg, tg), axis=0, keepdims=True)
        grp.append(m1 + jnp.where(n_eq >= 2.0, m1, second))

    vals = []
    for g in range(N_EXPERT_GROUPS):
        rank = jnp.zeros((1, tm), F32)
        for gp in range(N_EXPERT_GROUPS):
            if gp == g:
                continue
            beats = (grp[gp] >= grp[g]) if gp < g else (grp[gp] > grp[g])
            rank = rank + jnp.where(beats, 1.0, 0.0)
        vals.append(jnp.where(rank < float(TOPK_GROUPS), tiles[g], neg))

    sub = lax.broadcasted_iota(jnp.int32, (per_group, tm), 0).astype(F32)
    eidx = [sub + float(per_group * g) for g in range(N_EXPERT_GROUPS)]
    chosen = [jnp.zeros((per_group, tm), F32) for _ in range(N_EXPERT_GROUPS)]
    for _ in range(TOP_K):
        m = functools.reduce(jnp.maximum, [jnp.max(v, axis=0, keepdims=True) for v in vals])
        cand = [jnp.min(jnp.where(v == m, e, float(N_EXPERTS)), axis=0, keepdims=True)
                for v, e in zip(vals, eidx)]
        idx = functools.reduce(jnp.minimum, cand)
        for g in range(N_EXPERT_GROUPS):
            pick = eidx[g] == idx
            chosen[g] = jnp.where(pick, 1.0, chosen[g])
            vals[g] = jnp.where(pick, neg, vals[g])

    w = [jnp.where(c > 0.0, s, 0.0) for c, s in zip(chosen, sc_tiles)]
    denom = functools.reduce(jnp.add, [jnp.sum(t, axis=0, keepdims=True) for t in w])
    return jnp.concatenate([t / denom * ROUTED_SCALE for t in w], axis=0)


def _mix_kernel(x_ref, osb_ref, sga_ref, mb_ref, mod_ref, wsb_ref, wout_ref, g2_ref, wrh_ref, wrl_ref,
                rb_ref, wsgu_ref, wsd_ref, h2_ref, base_ref, gates_ref):
    m = mod_ref[0]
    gate1, shift2, scale2, gate2 = m[2:3], m[3:4], m[4:5], m[5:6]
    rows = x_ref.shape[0] // MIX_ROW_SPLITS
    for c in range(MIX_ROW_SPLITS):
        rs = slice(c * rows, (c + 1) * rows)
        ysb = _dot(osb_ref[rs, :], wsb_ref[...])
        merged = sga_ref[rs, :].astype(F32) * ysb + mb_ref[rs, :].astype(F32)
        x1 = x_ref[rs, :] + gate1 * _dot(merged.astype(BF16), wout_ref[...])

        r = lax.rsqrt(jnp.mean(x1 * x1, axis=-1, keepdims=True) + NORM_EPS)
        h2 = ((x1 * r) * g2_ref[...]) * (1.0 + scale2) + shift2
        h_hi, h_lo = _split_bf16(h2)
        h2_ref[rs, :] = h_hi

        su = _dot(h_hi, wsgu_ref[...])
        sh = _dot((_silu(su[:, :SHARED_DIM]) * su[:, SHARED_DIM:]).astype(BF16), wsd_ref[...])
        base_ref[rs, :] = x1 + gate2 * sh

        logits = _dot(h_hi, wrh_ref[...]) + _dot(h_hi, wrl_ref[...]) + _dot(h_lo, wrh_ref[...])
        scores = jax.nn.sigmoid(logits.T[:N_EXPERTS, :])
        gates = _route(scores + rb_ref[...], scores)
        gates_pad = jnp.concatenate([gates, jnp.zeros((LANES - N_EXPERTS, rows), F32)], axis=0)
        gates_ref[rs, :] = gates_pad.T


def _mix(x2, osb, sga, mb, mod3, wsb, wout, g2, wrh, wrl, rb, wsgu, wsd, seq):
    tm = TM_MIX
    tok = x2.shape[0]
    per_batch = seq // tm
    row = lambda i: (i, 0)
    const2 = lambda i: (0, 0)
    return pl.pallas_call(
        _mix_kernel,
        out_shape=(jax.ShapeDtypeStruct((tok, D_MODEL), BF16),
                   jax.ShapeDtypeStruct((tok, D_MODEL), F32),
                   jax.ShapeDtypeStruct((tok, LANES), F32)),
        grid=(tok // tm,),
        in_specs=[
            pl.BlockSpec((tm, D_MODEL), row),
            pl.BlockSpec((tm, SB_WIDTH), row),
            pl.BlockSpec((tm, D_MODEL), row),
            pl.BlockSpec((tm, D_MODEL), row),
            pl.BlockSpec((1, 6, D_MODEL), lambda i: (i // per_batch, 0, 0)),
            pl.BlockSpec(wsb.shape, const2),
            pl.BlockSpec(wout.shape, const2),
            pl.BlockSpec((1, D_MODEL), const2),
            pl.BlockSpec(wrh.shape, const2),
            pl.BlockSpec(wrl.shape, const2),
            pl.BlockSpec(rb.shape, const2),
            pl.BlockSpec(wsgu.shape, const2),
            pl.BlockSpec(wsd.shape, const2),
        ],
        out_specs=(pl.BlockSpec((tm, D_MODEL), row),
                   pl.BlockSpec((tm, D_MODEL), row),
                   pl.BlockSpec((tm, LANES), row)),
        compiler_params=pltpu.CompilerParams(dimension_semantics=("arbitrary",),
                                             vmem_limit_bytes=VMEM_LIMIT),
        name="mix",
    )(x2, osb, sga, mb, mod3, wsb, wout, g2, wrh, wrl, rb, wsgu, wsd)


def _moe_kernel(h_ref, g_ref, base_ref, mod_ref, wg_ref, wu_ref, wd_ref, o_ref, hm_ref):
    n_groups = N_EXPERTS // EXPERTS_PER_STEP
    s = pl.program_id(0)
    last = pl.num_programs(0) - 1
    group_up = jnp.minimum(s, last - 1) % n_groups
    group_dn = jnp.maximum(s - 1, 0) % n_groups
    width = EXPERTS_PER_STEP * EXPERT_DIM

    @pl.when(s == 0)
    def _():
        hm_ref[1] = jnp.zeros(hm_ref.shape[1:], BF16)

    @pl.when(group_dn == 0)
    def _():
        o_ref[...] = jnp.zeros_like(o_ref)

    wd = wd_ref[...].reshape(width, D_MODEL)
    wgu = jnp.concatenate([wg_ref[j] for j in range(EXPERTS_PER_STEP)]
                          + [wu_ref[j] for j in range(EXPERTS_PER_STEP)], axis=1)
    rows = h_ref.shape[0] // MOE_ROW_SPLITS
    lane = lax.broadcasted_iota(jnp.int32, (rows, LANES), 1)
    rd, wr = (s + 1) % 2, s % 2
    for c in range(MOE_ROW_SPLITS):
        rs = slice(c * rows, (c + 1) * rows)
        o_ref[rs, :] += _dot(hm_ref[rd, rs, :], wd)
        up = _dot(h_ref[rs, :], wgu)
        gates = g_ref[rs, :]
        parts = []
        for j in range(EXPERTS_PER_STEP):
            a = up[:, j * EXPERT_DIM:(j + 1) * EXPERT_DIM]
            b = up[:, width + j * EXPERT_DIM:width + (j + 1) * EXPERT_DIM]
            gate = jnp.sum(jnp.where(lane == group_up * EXPERTS_PER_STEP + j, gates, 0.0),
                           axis=1, keepdims=True)
            parts.append((_silu(a) * b * gate).astype(BF16))
        hm_ref[wr, rs, :] = jnp.concatenate(parts, axis=1)

    @pl.when(jnp.logical_and(s > 0, group_dn == n_groups - 1))
    def _():
        gate2 = mod_ref[0][5:6]
        o_ref[...] = base_ref[...] + gate2 * o_ref[...]


def _moe(h2, gates, base, mod3, wg, wu, wd, seq):
    tm = TM_MOE
    tok = h2.shape[0]
    per_batch = seq // tm
    g = EXPERTS_PER_STEP
    n_groups = N_EXPERTS // g
    n_steps = (tok // tm) * n_groups + 1
    up_step = lambda s: jnp.minimum(s, n_steps - 2)
    dn_step = lambda s: jnp.maximum(s - 1, 0)
    up_tile = lambda s: (up_step(s) // n_groups, 0)
    dn_tile = lambda s: (dn_step(s) // n_groups, 0)
    return pl.pallas_call(
        _moe_kernel,
        out_shape=jax.ShapeDtypeStruct((tok, D_MODEL), F32),
        grid=(n_steps,),
        in_specs=[
            pl.BlockSpec((tm, D_MODEL), up_tile, pipeline_mode=pl.Buffered(1)),
            pl.BlockSpec((tm, LANES), up_tile, pipeline_mode=pl.Buffered(1)),
            pl.BlockSpec((tm, D_MODEL), dn_tile, pipeline_mode=pl.Buffered(1)),
            pl.BlockSpec((1, 6, D_MODEL), lambda s: (dn_step(s) // n_groups // per_batch, 0, 0)),
            pl.BlockSpec((g, D_MODEL, EXPERT_DIM), lambda s: (up_step(s) % n_groups, 0, 0)),
            pl.BlockSpec((g, D_MODEL, EXPERT_DIM), lambda s: (up_step(s) % n_groups, 0, 0)),
            pl.BlockSpec((g, EXPERT_DIM, D_MODEL), lambda s: (dn_step(s) % n_groups, 0, 0)),
        ],
        out_specs=pl.BlockSpec((tm, D_MODEL), dn_tile),
        scratch_shapes=[pltpu.VMEM((2, tm, g * EXPERT_DIM), BF16)],
        compiler_params=pltpu.CompilerParams(dimension_semantics=("arbitrary",),
                                             vmem_limit_bytes=VMEM_LIMIT),
        name="moe",
    )(h2, gates, base, mod3, wg, wu, wd)


def _block_diag(blocks):
    n = len(blocks)
    rows = [jnp.concatenate([blocks[i] if i == j else jnp.zeros_like(blocks[j]) for j in range(n)], axis=1)
            for i in range(n)]
    return jnp.concatenate(rows, axis=0)


def _layer(x2, c_pad, batch, seq, w_ada, b_ada, g_norm1, w_in, g_q, g_k, w_sb_out, w_pool_group,
           pool_scale, w_pool_out, w_out, g_norm2, w_router, router_bias,
           w_exp_gate, w_exp_up, w_exp_down, w_sh_gate, w_sh_up, w_sh_down):
    mod = _ada(c_pad, w_ada, b_ada[None, :])
    mod3 = mod[:batch].reshape(batch, 6, D_MODEL)

    heads_per_tile = MXU_TILE // HEAD_DIM
    head_avg = np.kron(np.eye(heads_per_tile, dtype=np.float32),
                       np.full((HEAD_DIM, HEAD_DIM), 1.0 / HEAD_DIM, np.float32))
    head_avg2 = jnp.asarray(np.concatenate([head_avg, head_avg], axis=0), dtype=BF16)
    gq = (jnp.tile(g_q, SB_HEADS) * (HEAD_DIM ** -0.5 * LOG2_E))[None, :]
    gk = jnp.tile(g_k, SB_HEADS)[None, :]
    wpg = w_pool_group.astype(BF16)
    wpg_pairs = jnp.stack([_block_diag([wpg[2 * i], wpg[2 * i + 1]])
                           for i in range(len(POOL_WINDOWS) // 2)])
    q, k, v, sga, mb = _inproj(
        x2, mod3, g_norm1[None, :], w_in.astype(BF16), gq, gk, head_avg2,
        wpg_pairs, pool_scale[None, :], w_pool_out.astype(BF16), batch, seq)

    shape3 = (batch, seq, SB_WIDTH)
    osb = _attention(q.reshape(shape3), k.reshape(shape3), v.reshape(shape3)).reshape(batch * seq, SB_WIDTH)

    wr = jnp.pad(w_router, ((0, 0), (0, LANES - N_EXPERTS)))
    wr_hi = wr.astype(BF16)
    wr_lo = (wr - wr_hi.astype(F32)).astype(BF16)
    wsgu = jnp.concatenate([w_sh_gate, w_sh_up], axis=1).astype(BF16)
    h2, base, gates = _mix(
        x2, osb, sga, mb, mod3, w_sb_out.astype(BF16), w_out.astype(BF16), g_norm2[None, :],
        wr_hi, wr_lo, router_bias[:, None], wsgu, w_sh_down.astype(BF16), seq)

    return _moe(h2, gates, base, mod3, w_exp_gate.astype(BF16), w_exp_up.astype(BF16),
                w_exp_down.astype(BF16), seq)


def kernel(x, c, w_ada, b_ada, g_norm1, w_in, g_q, g_k, w_sb_out, w_pool_group, pool_scale, w_pool_out,
           w_out, g_norm2, w_router, router_bias, w_exp_gate, w_exp_up, w_exp_down,
           w_sh_gate, w_sh_up, w_sh_down):
    batch, seq, d = x.shape
    x2 = x.reshape(batch * seq, d)
    c_pad = jnp.pad(c, ((0, 8 - batch), (0, 0)))
    for l in range(w_ada.shape[0]):
        x2 = _layer(x2, c_pad, batch, seq, w_ada[l], b_ada[l], g_norm1[l], w_in[l], g_q[l], g_k[l],
                    w_sb_out[l], w_pool_group[l], pool_scale[l], w_pool_out[l], w_out[l], g_norm2[l],
                    w_router[l], router_bias[l], w_exp_gate[l], w_exp_up[l], w_exp_down[l],
                    w_sh_gate[l], w_sh_up[l], w_sh_down[l])
    return x2.reshape(batch, seq, d)
```

```python
import functools

import jax
import jax.numpy as jnp
import numpy as np
from jax import lax
from jax.experimental import pallas as pl
from jax.experimental.pallas import tpu as pltpu

D_MODEL = 1024
SB_HEADS = 8
HEAD_DIM = 64
SB_WIDTH = SB_HEADS * HEAD_DIM
POOL_WINDOWS = (2, 4, 8, 16)
POOL_WIDTH = 512
POOL_GROUP_DIM = POOL_WIDTH // len(POOL_WINDOWS)
MAX_WIN = max(POOL_WINDOWS)
N_EXPERTS = 64
EXPERT_DIM = 128
TOP_K = 8
N_EXPERT_GROUPS = 8
TOPK_GROUPS = 4
SHARED_DIM = 256
ROUTED_SCALE = 2.5
NORM_EPS = 1e-6

LANES = 128
MXU_TILE = 256
VMEM_LIMIT = 56 * 1024 * 1024

TM_IN = 1024
TM_MIX = 1024
MIX_ROW_SPLITS = 4
IN_ROW_SPLITS = 4
TM_MOE = 2048
MOE_ROW_SPLITS = 4
EXPERTS_PER_STEP = 4
ATTN_Q = 128
ATTN_UNROLL = 2
ATTN_K = 384
ATTN_SKIP_LOG2 = 128.0
SIGN_BIT = 0x80000000
MASKED_LOGIT = -1e30
LOG2_E = 1.4426950408889634

F32 = jnp.float32
BF16 = jnp.bfloat16


def _dot(a, b):
    return jnp.dot(a, b, preferred_element_type=F32)


def _split_bf16(x):
    hi = x.astype(BF16)
    lo = (x - hi.astype(F32)).astype(BF16)
    return hi, lo


def _silu(x):
    return x * jax.nn.sigmoid(x)


def _ada_kernel(c_ref, w_ref, b_ref, o_ref):
    c = c_ref[...]
    o_ref[...] = _dot(_silu(c).astype(BF16), w_ref[...].astype(BF16)) + b_ref[...]


def _ada(c_pad, w_ada, b_ada):
    n = w_ada.shape[1]
    tn = D_MODEL
    return pl.pallas_call(
        _ada_kernel,
        out_shape=jax.ShapeDtypeStruct((c_pad.shape[0], n), F32),
        grid=(n // tn,),
        in_specs=[
            pl.BlockSpec(c_pad.shape, lambda i: (0, 0)),
            pl.BlockSpec((D_MODEL, tn), lambda i: (0, i)),
            pl.BlockSpec((1, tn), lambda i: (0, i)),
        ],
        out_specs=pl.BlockSpec((c_pad.shape[0], tn), lambda i: (0, i)),
        compiler_params=pltpu.CompilerParams(dimension_semantics=("arbitrary",),
                                             vmem_limit_bytes=VMEM_LIMIT),
        name="ada",
    )(c_pad, w_ada, b_ada)


def _in_kernel(x_ref, mod_ref, g1_ref, w_ref, gq_ref, gk_ref, bd_ref, wpg_ref, ps_ref, wpo_ref,
               q_ref, k_ref, v_ref, sga_ref, mb_ref, ubuf):
    j = pl.program_id(1)
    tm = x_ref.shape[0]
    m = mod_ref[0]
    shift1, scale1 = m[0:1], m[1:2]

    def headnorm(t, g_ref):
        sq_hi, sq_lo = _split_bf16(t * t)
        half = bd_ref.shape[1]
        ms = jnp.concatenate(
            [_dot(jnp.concatenate([sq_hi[:, c:c + half], sq_lo[:, c:c + half]], axis=1), bd_ref[...])
             for c in range(0, SB_WIDTH, half)], axis=1)
        return (t * lax.rsqrt(ms + NORM_EPS)) * g_ref[...]

    @pl.when(j == 0)
    def _():
        ubuf[0:MAX_WIN, :] = jnp.zeros((MAX_WIN, POOL_WIDTH), F32)

    @pl.when(j > 0)
    def _():
        ubuf[0:MAX_WIN, :] = ubuf[tm:tm + MAX_WIN, :]

    rows = tm // IN_ROW_SPLITS
    u0 = 3 * SB_WIDTH
    g0 = u0 + POOL_WIDTH
    for c in range(IN_ROW_SPLITS):
        r0 = c * rows
        rs = slice(r0, r0 + rows)
        x = x_ref[rs, :]
        r = lax.rsqrt(jnp.mean(x * x, axis=-1, keepdims=True) + NORM_EPS)
        hb = (((x * r) * g1_ref[...]) * (1.0 + scale1) + shift1).astype(BF16)

        def proj(lo, hi):
            return _dot(hb, w_ref[:, lo:hi])

        q_ref[rs, :] = headnorm(proj(0, SB_WIDTH), gq_ref).astype(BF16)
        k_ref[rs, :] = headnorm(proj(SB_WIDTH, 2 * SB_WIDTH), gk_ref).astype(BF16)
        v_ref[rs, :] = proj(2 * SB_WIDTH, 3 * SB_WIDTH).astype(BF16)

        u = proj(u0, u0 + POOL_WIDTH)
        ubuf[MAX_WIN + r0:MAX_WIN + r0 + rows, :] = u
        t = j * tm + r0 + lax.broadcasted_iota(jnp.int32, (rows, POOL_GROUP_DIM), 0)
        mixed = []
        for gi, w in enumerate(POOL_WINDOWS):
            lo = gi * POOL_GROUP_DIM
            ug = u[:, lo:lo + POOL_GROUP_DIM]
            s = ug
            for i in range(1, w):
                s = s + ubuf[MAX_WIN + r0 - i:MAX_WIN + r0 - i + rows, lo:lo + POOL_GROUP_DIM]
            cnt = jnp.minimum(t + 1, w).astype(F32)
            mixed.append((s / cnt - ug).astype(BF16))
        parts = [_dot(jnp.concatenate(mixed[2 * i:2 * i + 2], axis=1), wpg_ref[i])
                 for i in range(len(POOL_WINDOWS) // 2)]
        pooled = jnp.concatenate(parts, axis=1) * ps_ref[...]
        ypool = _dot(pooled.astype(BF16), wpo_ref[...])

        sga_ref[rs, :] = jax.nn.sigmoid(proj(g0, g0 + D_MODEL)).astype(BF16)
        mb_ref[rs, :] = (jax.nn.sigmoid(proj(g0 + D_MODEL, g0 + 2 * D_MODEL)) * ypool).astype(BF16)


def _inproj(x2, mod3, g1, w_in, gq, gk, bd, wpg, ps, wpo, batch, seq):
    tm = TM_IN
    nj = seq // tm
    tok = x2.shape[0]
    row = lambda b, j: (b * nj + j, 0)
    const2 = lambda b, j: (0, 0)
    out_sd = lambda n: jax.ShapeDtypeStruct((tok, n), BF16)
    return pl.pallas_call(
        _in_kernel,
        out_shape=(out_sd(SB_WIDTH), out_sd(SB_WIDTH), out_sd(SB_WIDTH), out_sd(D_MODEL), out_sd(D_MODEL)),
        grid=(batch, nj),
        in_specs=[
            pl.BlockSpec((tm, D_MODEL), row),
            pl.BlockSpec((1, 6, D_MODEL), lambda b, j: (b, 0, 0)),
            pl.BlockSpec((1, D_MODEL), const2),
            pl.BlockSpec(w_in.shape, const2),
            pl.BlockSpec((1, SB_WIDTH), const2),
            pl.BlockSpec((1, SB_WIDTH), const2),
            pl.BlockSpec(bd.shape, const2),
            pl.BlockSpec(wpg.shape, lambda b, j: (0, 0, 0)),
            pl.BlockSpec((1, POOL_WIDTH), const2),
            pl.BlockSpec((POOL_WIDTH, D_MODEL), const2),
        ],
        out_specs=(
            pl.BlockSpec((tm, SB_WIDTH), row),
            pl.BlockSpec((tm, SB_WIDTH), row),
            pl.BlockSpec((tm, SB_WIDTH), row),
            pl.BlockSpec((tm, D_MODEL), row),
            pl.BlockSpec((tm, D_MODEL), row),
        ),
        scratch_shapes=[pltpu.VMEM((MAX_WIN + tm, POOL_WIDTH), F32)],
        compiler_params=pltpu.CompilerParams(dimension_semantics=("arbitrary", "arbitrary"),
                                             vmem_limit_bytes=VMEM_LIMIT),
        name="inproj",
    )(x2, mod3, g1, w_in, gq, gk, bd, wpg, ps, wpo)


def _attn_kernel(q_ref, k_ref, v_ref, o_ref, tri_ref):
    tq, tk = ATTN_Q, ATTN_K
    seq = q_ref.shape[1]
    n_pairs = q_ref.shape[2] // LANES
    row = lax.broadcasted_iota(jnp.int32, (tk, tk), 0)
    col = lax.broadcasted_iota(jnp.int32, (tk, tk), 1)
    tri_ref[...] = jnp.where(row >= col, 1.0, 0.0).astype(BF16)
    kcol = lax.broadcasted_iota(jnp.int32, (tq, tk), 1)
    col_minus_row = kcol - lax.broadcasted_iota(jnp.int32, (tq, tk), 0)
    first_head = lax.broadcasted_iota(jnp.int32, (tq, LANES), 1) < HEAD_DIM

    n_heads = 2 * n_pairs

    def window(q_pairs, start, mask, carry=None, accs=None):
        zs = []
        for p in range(n_pairs):
            kb = k_ref[0, pl.ds(start, tk), p * LANES:(p + 1) * LANES]
            zs.append(lax.dot_general(q_pairs[p], kb, (((1,), (1,)), ((), ())), preferred_element_type=F32))
        z = jnp.concatenate(zs, axis=0).reshape(n_heads, tq, tk)
        z = jnp.where(mask[None], z, MASKED_LOGIT)
        neg_abs = lax.bitcast_convert_type(lax.bitcast_convert_type(z, jnp.uint32) | jnp.uint32(SIGN_BIT), F32)
        fail = jnp.maximum(z, 0.0) + jnp.log2(1.0 + jnp.exp2(neg_abs))
        spent = _dot(fail.astype(BF16).reshape(n_heads * tq, tk), tri_ref[...]).reshape(n_heads, tq, tk)
        log_w = z - spent
        total = spent[:, :, 0:1]
        if carry is not None:
            log_w = log_w - carry
            total = total + carry
        a = jnp.exp2(log_w).astype(BF16)
        new_accs = []
        for p in range(n_pairs):
            vb = v_ref[0, pl.ds(start, tk), p * LANES:(p + 1) * LANES]
            av = _dot(a[2 * p:2 * p + 2].reshape(2 * tq, tk), vb)
            for hh, part in enumerate((av[:tq], av[tq:])):
                new_accs.append(part if accs is None else accs[2 * p + hh] + part)
        return total, tuple(new_accs)

    def q_body(it, _):
        blocks = []
        for u in range(ATTN_UNROLL):
            qs = pl.multiple_of((it * ATTN_UNROLL + u) * tq, tq)
            ws = pl.multiple_of(jnp.maximum(qs - (tk - tq), 0), tq)
            causal = col_minus_row < (qs - ws)
            q_pairs = []
            for p in range(n_pairs):
                q2 = q_ref[0, pl.ds(qs, tq), p * LANES:(p + 1) * LANES]
                zero = jnp.zeros_like(q2)
                q_pairs.append(jnp.concatenate([jnp.where(first_head, q2, zero),
                                                jnp.where(first_head, zero, q2)], axis=0))
            carry, accs = window(q_pairs, ws, causal)
            blocks.append((qs, ws, q_pairs, carry, accs))

        for qs, ws, q_pairs, carry, accs in blocks:
            def cond(st):
                start, carry, _ = st
                return jnp.logical_and(start > 0, jnp.min(carry) < ATTN_SKIP_LOG2)

            def body(st, q_pairs=q_pairs):
                start, carry, accs = st
                nxt = pl.multiple_of(jnp.maximum(start - tk, 0), tq)
                carry, accs = window(q_pairs, nxt, kcol < (start - nxt), carry, accs)
                return nxt, carry, accs

            _, _, accs = lax.while_loop(cond, body, (ws, carry, accs))
            o = jnp.concatenate([jnp.where(first_head, accs[2 * p], accs[2 * p + 1])
                                 for p in range(n_pairs)], axis=1)
            o_ref[0, pl.ds(qs, tq), :] = o.astype(BF16)
        return 0

    lax.fori_loop(0, seq // (tq * ATTN_UNROLL), q_body, 0)


def _attention(q3, k3, v3):
    batch, seq, width = q3.shape
    spec = pl.BlockSpec((1, seq, width), lambda b: (b, 0, 0))
    return pl.pallas_call(
        _attn_kernel,
        out_shape=jax.ShapeDtypeStruct((batch, seq, width), BF16),
        grid=(batch,),
        in_specs=[spec, spec, spec],
        out_specs=spec,
        scratch_shapes=[pltpu.VMEM((ATTN_K, ATTN_K), BF16)],
        compiler_params=pltpu.CompilerParams(dimension_semantics=("arbitrary",),
                                             vmem_limit_bytes=VMEM_LIMIT),
        name="attn",
    )(q3, k3, v3)


def _route(sel, scores):
    per_group = N_EXPERTS // N_EXPERT_GROUPS
    tm = sel.shape[1]
    neg = -jnp.inf
    tiles = [sel[per_group * g:per_group * (g + 1), :] for g in range(N_EXPERT_GROUPS)]
    sc_tiles = [scores[per_group * g:per_group * (g + 1), :] for g in range(N_EXPERT_GROUPS)]

    grp = []
    for tg in tiles:
        m1 = jnp.max(tg, axis=0, keepdims=True)
        eq = tg == m1
        n_eq = jnp.sum(jnp.where(eq, 1.0, 0.0), axis=0, keepdims=True)
        second = jnp.max(jnp.where(eq, neg, tg), axis=0, keepdims=True)
        grp.append(m1 + jnp.where(n_eq >= 2.0, m1, second))

    vals = []
    for g in range(N_EXPERT_GROUPS):
        rank = jnp.zeros((1, tm), F32)
        for gp in range(N_EXPERT_GROUPS):
            if gp == g:
                continue
            beats = (grp[gp] >= grp[g]) if gp < g else (grp[gp] > grp[g])
            rank = rank + jnp.where(beats, 1.0, 0.0)
        vals.append(jnp.where(rank < float(TOPK_GROUPS), tiles[g], neg))

    sub = lax.broadcasted_iota(jnp.int32, (per_group, tm), 0).astype(F32)
    eidx = [sub + float(per_group * g) for g in range(N_EXPERT_GROUPS)]
    chosen = [jnp.zeros((per_group, tm), F32) for _ in range(N_EXPERT_GROUPS)]
    for _ in range(TOP_K):
        m = functools.reduce(jnp.maximum, [jnp.max(v, axis=0, keepdims=True) for v in vals])
        cand = [jnp.min(jnp.where(v == m, e, float(N_EXPERTS)), axis=0, keepdims=True)
                for v, e in zip(vals, eidx)]
        idx = functools.reduce(jnp.minimum, cand)
        for g in range(N_EXPERT_GROUPS):
            pick = eidx[g] == idx
            chosen[g] = jnp.where(pick, 1.0, chosen[g])
            vals[g] = jnp.where(pick, neg, vals[g])

    w = [jnp.where(c > 0.0, s, 0.0) for c, s in zip(chosen, sc_tiles)]
    denom = functools.reduce(jnp.add, [jnp.sum(t, axis=0, keepdims=True) for t in w])
    return jnp.concatenate([t / denom * ROUTED_SCALE for t in w], axis=0)


def _mix_kernel(x_ref, osb_ref, sga_ref, mb_ref, mod_ref, wsb_ref, wout_ref, g2_ref, wrh_ref, wrl_ref,
                rb_ref, wsgu_ref, wsd_ref, h2_ref, base_ref, gates_ref):
    m = mod_ref[0]
    gate1, shift2, scale2, gate2 = m[2:3], m[3:4], m[4:5], m[5:6]
    rows = x_ref.shape[0] // MIX_ROW_SPLITS
    for c in range(MIX_ROW_SPLITS):
        rs = slice(c * rows, (c + 1) * rows)
        ysb = _dot(osb_ref[rs, :], wsb_ref[...])
        merged = sga_ref[rs, :].astype(F32) * ysb + mb_ref[rs, :].astype(F32)
        x1 = x_ref[rs, :] + gate1 * _dot(merged.astype(BF16), wout_ref[...])

        r = lax.rsqrt(jnp.mean(x1 * x1, axis=-1, keepdims=True) + NORM_EPS)
        h2 = ((x1 * r) * g2_ref[...]) * (1.0 + scale2) + shift2
        h_hi, h_lo = _split_bf16(h2)
        h2_ref[rs, :] = h_hi

        su = _dot(h_hi, wsgu_ref[...])
        sh = _dot((_silu(su[:, :SHARED_DIM]) * su[:, SHARED_DIM:]).astype(BF16), wsd_ref[...])
        base_ref[rs, :] = x1 + gate2 * sh

        logits = _dot(h_hi, wrh_ref[...]) + _dot(h_hi, wrl_ref[...]) + _dot(h_lo, wrh_ref[...])
        scores = jax.nn.sigmoid(logits.T[:N_EXPERTS, :])
        gates = _route(scores + rb_ref[...], scores)
        gates_pad = jnp.concatenate([gates, jnp.zeros((LANES - N_EXPERTS, rows), F32)], axis=0)
        gates_ref[rs, :] = gates_pad.T


def _mix(x2, osb, sga, mb, mod3, wsb, wout, g2, wrh, wrl, rb, wsgu, wsd, seq):
    tm = TM_MIX
    tok = x2.shape[0]
    per_batch = seq // tm
    row = lambda i: (i, 0)
    const2 = lambda i: (0, 0)
    return pl.pallas_call(
        _mix_kernel,
        out_shape=(jax.ShapeDtypeStruct((tok, D_MODEL), BF16),
                   jax.ShapeDtypeStruct((tok, D_MODEL), F32),
                   jax.ShapeDtypeStruct((tok, LANES), F32)),
        grid=(tok // tm,),
        in_specs=[
            pl.BlockSpec((tm, D_MODEL), row),
            pl.BlockSpec((tm, SB_WIDTH), row),
            pl.BlockSpec((tm, D_MODEL), row),
            pl.BlockSpec((tm, D_MODEL), row),
            pl.BlockSpec((1, 6, D_MODEL), lambda i: (i // per_batch, 0, 0)),
            pl.BlockSpec(wsb.shape, const2),
            pl.BlockSpec(wout.shape, const2),
            pl.BlockSpec((1, D_MODEL), const2),
            pl.BlockSpec(wrh.shape, const2),
            pl.BlockSpec(wrl.shape, const2),
            pl.BlockSpec(rb.shape, const2),
            pl.BlockSpec(wsgu.shape, const2),
            pl.BlockSpec(wsd.shape, const2),
        ],
        out_specs=(pl.BlockSpec((tm, D_MODEL), row),
                   pl.BlockSpec((tm, D_MODEL), row),
                   pl.BlockSpec((tm, LANES), row)),
        compiler_params=pltpu.CompilerParams(dimension_semantics=("arbitrary",),
                                             vmem_limit_bytes=VMEM_LIMIT),
        name="mix",
    )(x2, osb, sga, mb, mod3, wsb, wout, g2, wrh, wrl, rb, wsgu, wsd)


def _moe_kernel(h_ref, g_ref, base_ref, mod_ref, wg_ref, wu_ref, wd_ref, o_ref, hm_ref):
    n_groups = N_EXPERTS // EXPERTS_PER_STEP
    s = pl.program_id(0)
    last = pl.num_programs(0) - 1
    group_up = jnp.minimum(s, last - 1) % n_groups
    group_dn = jnp.maximum(s - 1, 0) % n_groups
    width = EXPERTS_PER_STEP * EXPERT_DIM

    @pl.when(s == 0)
    def _():
        hm_ref[1] = jnp.zeros(hm_ref.shape[1:], BF16)

    @pl.when(group_dn == 0)
    def _():
        o_ref[...] = jnp.zeros_like(o_ref)

    wd = wd_ref[...].astype(BF16).reshape(width, D_MODEL)
    wgu = jnp.concatenate([wg_ref[j].astype(BF16) for j in range(EXPERTS_PER_STEP)]
                          + [wu_ref[j].astype(BF16) for j in range(EXPERTS_PER_STEP)], axis=1)
    rows = h_ref.shape[0] // MOE_ROW_SPLITS
    lane = lax.broadcasted_iota(jnp.int32, (rows, LANES), 1)
    rd, wr = (s + 1) % 2, s % 2
    for c in range(MOE_ROW_SPLITS):
        rs = slice(c * rows, (c + 1) * rows)
        o_ref[rs, :] += _dot(hm_ref[rd, rs, :], wd)
    for c in range(MOE_ROW_SPLITS):
        rs = slice(c * rows, (c + 1) * rows)
        up = _dot(h_ref[rs, :], wgu)
        gates = g_ref[rs, :]
        parts = []
        for j in range(EXPERTS_PER_STEP):
            a = up[:, j * EXPERT_DIM:(j + 1) * EXPERT_DIM]
            b = up[:, width + j * EXPERT_DIM:width + (j + 1) * EXPERT_DIM]
            gate = jnp.sum(jnp.where(lane == group_up * EXPERTS_PER_STEP + j, gates, 0.0),
                           axis=1, keepdims=True)
            parts.append((_silu(a) * b * gate).astype(BF16))
        hm_ref[wr, rs, :] = jnp.concatenate(parts, axis=1)

    @pl.when(jnp.logical_and(s > 0, group_dn == n_groups - 1))
    def _():
        gate2 = mod_ref[0][5:6]
        o_ref[...] = base_ref[...] + gate2 * o_ref[...]


def _moe(h2, gates, base, mod3, wg, wu, wd, seq):
    tm = TM_MOE
    tok = h2.shape[0]
    per_batch = seq // tm
    g = EXPERTS_PER_STEP
    n_groups = N_EXPERTS // g
    n_steps = (tok // tm) * n_groups + 1
    up_step = lambda s: jnp.minimum(s, n_steps - 2)
    dn_step = lambda s: jnp.maximum(s - 1, 0)
    up_tile = lambda s: (up_step(s) // n_groups, 0)
    dn_tile = lambda s: (dn_step(s) // n_groups, 0)
    return pl.pallas_call(
        _moe_kernel,
        out_shape=jax.ShapeDtypeStruct((tok, D_MODEL), F32),
        grid=(n_steps,),
        in_specs=[
            pl.BlockSpec((tm, D_MODEL), up_tile, pipeline_mode=pl.Buffered(1)),
            pl.BlockSpec((tm, LANES), up_tile, pipeline_mode=pl.Buffered(1)),
            pl.BlockSpec((tm, D_MODEL), dn_tile, pipeline_mode=pl.Buffered(1)),
            pl.BlockSpec((1, 6, D_MODEL), lambda s: (dn_step(s) // n_groups // per_batch, 0, 0)),
            pl.BlockSpec((g, D_MODEL, EXPERT_DIM), lambda s: (up_step(s) % n_groups, 0, 0)),
            pl.BlockSpec((g, D_MODEL, EXPERT_DIM), lambda s: (up_step(s) % n_groups, 0, 0)),
            pl.BlockSpec((g, EXPERT_DIM, D_MODEL), lambda s: (dn_step(s) % n_groups, 0, 0)),
        ],
        out_specs=pl.BlockSpec((tm, D_MODEL), dn_tile),
        scratch_shapes=[pltpu.VMEM((2, tm, g * EXPERT_DIM), BF16)],
        compiler_params=pltpu.CompilerParams(dimension_semantics=("arbitrary",),
                                             vmem_limit_bytes=VMEM_LIMIT),
        name="moe",
    )(h2, gates, base, mod3, wg, wu, wd)


def _block_diag(blocks):
    n = len(blocks)
    rows = [jnp.concatenate([blocks[i] if i == j else jnp.zeros_like(blocks[j]) for j in range(n)], axis=1)
            for i in range(n)]
    return jnp.concatenate(rows, axis=0)


def _layer(x2, c_pad, batch, seq, w_ada, b_ada, g_norm1, w_in, g_q, g_k, w_sb_out, w_pool_group,
           pool_scale, w_pool_out, w_out, g_norm2, w_router, router_bias,
           w_exp_gate, w_exp_up, w_exp_down, w_sh_gate, w_sh_up, w_sh_down):
    mod = _ada(c_pad, w_ada, b_ada[None, :])
    mod3 = mod[:batch].reshape(batch, 6, D_MODEL)

    heads_per_tile = MXU_TILE // HEAD_DIM
    head_avg = np.kron(np.eye(heads_per_tile, dtype=np.float32),
                       np.full((HEAD_DIM, HEAD_DIM), 1.0 / HEAD_DIM, np.float32))
    head_avg2 = jnp.asarray(np.concatenate([head_avg, head_avg], axis=0), dtype=BF16)
    gq = (jnp.tile(g_q, SB_HEADS) * (HEAD_DIM ** -0.5 * LOG2_E))[None, :]
    gk = jnp.tile(g_k, SB_HEADS)[None, :]
    wpg = w_pool_group.astype(BF16)
    wpg_pairs = jnp.stack([_block_diag([wpg[2 * i], wpg[2 * i + 1]])
                           for i in range(len(POOL_WINDOWS) // 2)])
    q, k, v, sga, mb = _inproj(
        x2, mod3, g_norm1[None, :], w_in.astype(BF16), gq, gk, head_avg2,
        wpg_pairs, pool_scale[None, :], w_pool_out.astype(BF16), batch, seq)

    shape3 = (batch, seq, SB_WIDTH)
    osb = _attention(q.reshape(shape3), k.reshape(shape3), v.reshape(shape3)).reshape(batch * seq, SB_WIDTH)

    wr = jnp.pad(w_router, ((0, 0), (0, LANES - N_EXPERTS)))
    wr_hi = wr.astype(BF16)
    wr_lo = (wr - wr_hi.astype(F32)).astype(BF16)
    wsgu = jnp.concatenate([w_sh_gate, w_sh_up], axis=1).astype(BF16)
    h2, base, gates = _mix(
        x2, osb, sga, mb, mod3, w_sb_out.astype(BF16), w_out.astype(BF16), g_norm2[None, :],
        wr_hi, wr_lo, router_bias[:, None], wsgu, w_sh_down.astype(BF16), seq)

    return _moe(h2, gates, base, mod3, w_exp_gate, w_exp_up, w_exp_down, seq)


def kernel(x, c, w_ada, b_ada, g_norm1, w_in, g_q, g_k, w_sb_out, w_pool_group, pool_scale, w_pool_out,
           w_out, g_norm2, w_router, router_bias, w_exp_gate, w_exp_up, w_exp_down,
           w_sh_gate, w_sh_up, w_sh_down):
    batch, seq, d = x.shape
    x2 = x.reshape(batch * seq, d)
    c_pad = jnp.pad(c, ((0, 8 - batch), (0, 0)))
    for l in range(w_ada.shape[0]):
        x2 = _layer(x2, c_pad, batch, seq, w_ada[l], b_ada[l], g_norm1[l], w_in[l], g_q[l], g_k[l],
                    w_sb_out[l], w_pool_group[l], pool_scale[l], w_pool_out[l], w_out[l], g_norm2[l],
                    w_router[l], router_bias[l], w_exp_gate[l], w_exp_up[l], w_exp_down[l],
                    w_sh_gate[l], w_sh_up[l], w_sh_down[l])
    return x2.reshape(batch, seq, d)
```

```python
import functools

import jax
import jax.numpy as jnp
import numpy as np
from jax import lax
from jax.experimental import pallas as pl
from jax.experimental.pallas import tpu as pltpu

D_MODEL = 1024
SB_HEADS = 8
HEAD_DIM = 64
SB_WIDTH = SB_HEADS * HEAD_DIM
POOL_WINDOWS = (2, 4, 8, 16)
POOL_WIDTH = 512
POOL_GROUP_DIM = POOL_WIDTH // len(POOL_WINDOWS)
MAX_WIN = max(POOL_WINDOWS)
N_EXPERTS = 64
EXPERT_DIM = 128
TOP_K = 8
N_EXPERT_GROUPS = 8
TOPK_GROUPS = 4
SHARED_DIM = 256
ROUTED_SCALE = 2.5
NORM_EPS = 1e-6

LANES = 128
MXU_TILE = 256
VMEM_LIMIT = 56 * 1024 * 1024

TM_IN = 1024
TM_MIX = 1024
MIX_ROW_SPLITS = 4
IN_ROW_SPLITS = 4
TM_MOE = 2048
MOE_ROW_SPLITS = 4
EXPERTS_PER_STEP = 4
ATTN_Q = 128
ATTN_UNROLL = 2
ATTN_K = 384
ATTN_SKIP_LOG2 = 128.0
MASKED_LOGIT = -1e30
LOG2_E = 1.4426950408889634

F32 = jnp.float32
BF16 = jnp.bfloat16


def _dot(a, b):
    return jnp.dot(a, b, preferred_element_type=F32)


def _split_bf16(x):
    hi = x.astype(BF16)
    lo = (x - hi.astype(F32)).astype(BF16)
    return hi, lo


def _silu(x):
    return x * jax.nn.sigmoid(x)


def _ada_kernel(c_ref, w_ref, b_ref, o_ref):
    c = c_ref[...]
    o_ref[...] = _dot(_silu(c).astype(BF16), w_ref[...].astype(BF16)) + b_ref[...]


def _ada(c_pad, w_ada, b_ada):
    n = w_ada.shape[1]
    tn = D_MODEL
    return pl.pallas_call(
        _ada_kernel,
        out_shape=jax.ShapeDtypeStruct((c_pad.shape[0], n), F32),
        grid=(n // tn,),
        in_specs=[
            pl.BlockSpec(c_pad.shape, lambda i: (0, 0)),
            pl.BlockSpec((D_MODEL, tn), lambda i: (0, i)),
            pl.BlockSpec((1, tn), lambda i: (0, i)),
        ],
        out_specs=pl.BlockSpec((c_pad.shape[0], tn), lambda i: (0, i)),
        compiler_params=pltpu.CompilerParams(dimension_semantics=("arbitrary",),
                                             vmem_limit_bytes=VMEM_LIMIT),
        name="ada",
    )(c_pad, w_ada, b_ada)


def _in_kernel(x_ref, mod_ref, g1_ref, w_ref, gq_ref, gk_ref, bd_ref, wpg_ref, ps_ref, wpo_ref,
               q_ref, k_ref, v_ref, sga_ref, mb_ref, ubuf):
    j = pl.program_id(1)
    tm = x_ref.shape[0]
    m = mod_ref[0]
    shift1, scale1 = m[0:1], m[1:2]

    def headnorm(t, g_ref):
        sq = (t * t).astype(BF16)
        half = bd_ref.shape[1]
        ms = jnp.concatenate([_dot(sq[:, c:c + half], bd_ref[...]) for c in range(0, SB_WIDTH, half)], axis=1)
        return (t * lax.rsqrt(ms + NORM_EPS)) * g_ref[...]

    @pl.when(j == 0)
    def _():
        ubuf[0:MAX_WIN, :] = jnp.zeros((MAX_WIN, POOL_WIDTH), F32)

    @pl.when(j > 0)
    def _():
        ubuf[0:MAX_WIN, :] = ubuf[tm:tm + MAX_WIN, :]

    rows = tm // IN_ROW_SPLITS
    u0 = 3 * SB_WIDTH
    g0 = u0 + POOL_WIDTH
    for c in range(IN_ROW_SPLITS):
        r0 = c * rows
        rs = slice(r0, r0 + rows)
        x = x_ref[rs, :]
        r = lax.rsqrt(jnp.mean(x * x, axis=-1, keepdims=True) + NORM_EPS)
        hb = (((x * r) * g1_ref[...]) * (1.0 + scale1) + shift1).astype(BF16)

        def proj(lo, hi):
            return _dot(hb, w_ref[:, lo:hi])

        q_ref[rs, :] = headnorm(proj(0, SB_WIDTH), gq_ref).astype(BF16)
        k_ref[rs, :] = headnorm(proj(SB_WIDTH, 2 * SB_WIDTH), gk_ref).astype(BF16)
        v_ref[rs, :] = proj(2 * SB_WIDTH, 3 * SB_WIDTH).astype(BF16)

        u = proj(u0, u0 + POOL_WIDTH)
        ubuf[MAX_WIN + r0:MAX_WIN + r0 + rows, :] = u
        t = j * tm + r0 + lax.broadcasted_iota(jnp.int32, (rows, POOL_GROUP_DIM), 0)
        mixed = []
        for gi, w in enumerate(POOL_WINDOWS):
            lo = gi * POOL_GROUP_DIM
            ug = u[:, lo:lo + POOL_GROUP_DIM]
            s = ug
            for i in range(1, w):
                s = s + ubuf[MAX_WIN + r0 - i:MAX_WIN + r0 - i + rows, lo:lo + POOL_GROUP_DIM]
            cnt = jnp.minimum(t + 1, w).astype(F32)
            mixed.append((s / cnt - ug).astype(BF16))
        parts = [_dot(jnp.concatenate(mixed[2 * i:2 * i + 2], axis=1), wpg_ref[i])
                 for i in range(len(POOL_WINDOWS) // 2)]
        pooled = jnp.concatenate(parts, axis=1) * ps_ref[...]
        ypool = _dot(pooled.astype(BF16), wpo_ref[...])

        sga_ref[rs, :] = jax.nn.sigmoid(proj(g0, g0 + D_MODEL)).astype(BF16)
        mb_ref[rs, :] = (jax.nn.sigmoid(proj(g0 + D_MODEL, g0 + 2 * D_MODEL)) * ypool).astype(BF16)


def _inproj(x2, mod3, g1, w_in, gq, gk, bd, wpg, ps, wpo, batch, seq):
    tm = TM_IN
    nj = seq // tm
    tok = x2.shape[0]
    row = lambda b, j: (b * nj + j, 0)
    const2 = lambda b, j: (0, 0)
    out_sd = lambda n: jax.ShapeDtypeStruct((tok, n), BF16)
    return pl.pallas_call(
        _in_kernel,
        out_shape=(out_sd(SB_WIDTH), out_sd(SB_WIDTH), out_sd(SB_WIDTH), out_sd(D_MODEL), out_sd(D_MODEL)),
        grid=(batch, nj),
        in_specs=[
            pl.BlockSpec((tm, D_MODEL), row),
            pl.BlockSpec((1, 6, D_MODEL), lambda b, j: (b, 0, 0)),
            pl.BlockSpec((1, D_MODEL), const2),
            pl.BlockSpec(w_in.shape, const2),
            pl.BlockSpec((1, SB_WIDTH), const2),
            pl.BlockSpec((1, SB_WIDTH), const2),
            pl.BlockSpec(bd.shape, const2),
            pl.BlockSpec(wpg.shape, lambda b, j: (0, 0, 0)),
            pl.BlockSpec((1, POOL_WIDTH), const2),
            pl.BlockSpec((POOL_WIDTH, D_MODEL), const2),
        ],
        out_specs=(
            pl.BlockSpec((tm, SB_WIDTH), row),
            pl.BlockSpec((tm, SB_WIDTH), row),
            pl.BlockSpec((tm, SB_WIDTH), row),
            pl.BlockSpec((tm, D_MODEL), row),
            pl.BlockSpec((tm, D_MODEL), row),
        ),
        scratch_shapes=[pltpu.VMEM((MAX_WIN + tm, POOL_WIDTH), F32)],
        compiler_params=pltpu.CompilerParams(dimension_semantics=("arbitrary", "arbitrary"),
                                             vmem_limit_bytes=VMEM_LIMIT),
        name="inproj",
    )(x2, mod3, g1, w_in, gq, gk, bd, wpg, ps, wpo)


def _attn_kernel(q_ref, k_ref, v_ref, o_ref, tri_ref):
    tq, tk = ATTN_Q, ATTN_K
    seq = q_ref.shape[1]
    n_pairs = q_ref.shape[2] // LANES
    row = lax.broadcasted_iota(jnp.int32, (tk, tk), 0)
    col = lax.broadcasted_iota(jnp.int32, (tk, tk), 1)
    tri_ref[...] = jnp.where(row >= col, 1.0, 0.0).astype(BF16)
    kcol = lax.broadcasted_iota(jnp.int32, (tq, tk), 1)
    col_minus_row = kcol - lax.broadcasted_iota(jnp.int32, (tq, tk), 0)
    first_head = lax.broadcasted_iota(jnp.int32, (tq, LANES), 1) < HEAD_DIM

    n_heads = 2 * n_pairs

    def window(q_pairs, start, mask, carry=None, accs=None):
        zs = []
        for p in range(n_pairs):
            kb = k_ref[0, pl.ds(start, tk), p * LANES:(p + 1) * LANES]
            zs.append(lax.dot_general(q_pairs[p], kb, (((1,), (1,)), ((), ())), preferred_element_type=F32))
        z = jnp.concatenate(zs, axis=0).reshape(n_heads, tq, tk)
        z = jnp.where(mask[None], z, MASKED_LOGIT)
        fail = jnp.maximum(z, 0.0) + jnp.log2(1.0 + jnp.exp2(-jnp.abs(z)))
        spent = _dot(fail.astype(BF16).reshape(n_heads * tq, tk), tri_ref[...]).reshape(n_heads, tq, tk)
        log_w = z - spent
        total = spent[:, :, 0:1]
        if carry is not None:
            log_w = log_w - carry
            total = total + carry
        a = jnp.exp2(log_w).astype(BF16)
        new_accs = []
        for p in range(n_pairs):
            vb = v_ref[0, pl.ds(start, tk), p * LANES:(p + 1) * LANES]
            av = _dot(a[2 * p:2 * p + 2].reshape(2 * tq, tk), vb)
            for hh, part in enumerate((av[:tq], av[tq:])):
                new_accs.append(part if accs is None else accs[2 * p + hh] + part)
        return total, tuple(new_accs)

    def q_body(it, _):
        blocks = []
        for u in range(ATTN_UNROLL):
            qs = pl.multiple_of((it * ATTN_UNROLL + u) * tq, tq)
            ws = pl.multiple_of(jnp.maximum(qs - (tk - tq), 0), tq)
            causal = col_minus_row < (qs - ws)
            q_pairs = []
            for p in range(n_pairs):
                q2 = q_ref[0, pl.ds(qs, tq), p * LANES:(p + 1) * LANES]
                zero = jnp.zeros_like(q2)
                q_pairs.append(jnp.concatenate([jnp.where(first_head, q2, zero),
                                                jnp.where(first_head, zero, q2)], axis=0))
            carry, accs = window(q_pairs, ws, causal)
            blocks.append((qs, ws, q_pairs, carry, accs))

        for qs, ws, q_pairs, carry, accs in blocks:
            def cond(st):
                start, carry, _ = st
                return jnp.logical_and(start > 0, jnp.min(carry) < ATTN_SKIP_LOG2)

            def body(st, q_pairs=q_pairs):
                start, carry, accs = st
                nxt = pl.multiple_of(jnp.maximum(start - tk, 0), tq)
                carry, accs = window(q_pairs, nxt, kcol < (start - nxt), carry, accs)
                return nxt, carry, accs

            _, _, accs = lax.while_loop(cond, body, (ws, carry, accs))
            o = jnp.concatenate([jnp.where(first_head, accs[2 * p], accs[2 * p + 1])
                                 for p in range(n_pairs)], axis=1)
            o_ref[0, pl.ds(qs, tq), :] = o.astype(BF16)
        return 0

    lax.fori_loop(0, seq // (tq * ATTN_UNROLL), q_body, 0)


def _attention(q3, k3, v3):
    batch, seq, width = q3.shape
    spec = pl.BlockSpec((1, seq, width), lambda b: (b, 0, 0))
    return pl.pallas_call(
        _attn_kernel,
        out_shape=jax.ShapeDtypeStruct((batch, seq, width), BF16),
        grid=(batch,),
        in_specs=[spec, spec, spec],
        out_specs=spec,
        scratch_shapes=[pltpu.VMEM((ATTN_K, ATTN_K), BF16)],
        compiler_params=pltpu.CompilerParams(dimension_semantics=("arbitrary",),
                                             vmem_limit_bytes=VMEM_LIMIT),
        name="attn",
    )(q3, k3, v3)


def _route(sel, scores):
    per_group = N_EXPERTS // N_EXPERT_GROUPS
    tm = sel.shape[1]
    neg = -jnp.inf
    tiles = [sel[per_group * g:per_group * (g + 1), :] for g in range(N_EXPERT_GROUPS)]
    sc_tiles = [scores[per_group * g:per_group * (g + 1), :] for g in range(N_EXPERT_GROUPS)]

    grp = []
    for tg in tiles:
        m1 = jnp.max(tg, axis=0, keepdims=True)
        eq = tg == m1
        n_eq = jnp.sum(jnp.where(eq, 1.0, 0.0), axis=0, keepdims=True)
        second = jnp.max(jnp.where(eq, neg, tg), axis=0, keepdims=True)
        grp.append(m1 + jnp.where(n_eq >= 2.0, m1, second))

    vals = []
    for g in range(N_EXPERT_GROUPS):
        rank = jnp.zeros((1, tm), F32)
        for gp in range(N_EXPERT_GROUPS):
            if gp == g:
                continue
            beats = (grp[gp] >= grp[g]) if gp < g else (grp[gp] > grp[g])
            rank = rank + jnp.where(beats, 1.0, 0.0)
        vals.append(jnp.where(rank < float(TOPK_GROUPS), tiles[g], neg))

    sub = lax.broadcasted_iota(jnp.int32, (per_group, tm), 0).astype(F32)
    eidx = [sub + float(per_group * g) for g in range(N_EXPERT_GROUPS)]
    chosen = [jnp.zeros((per_group, tm), F32) for _ in range(N_EXPERT_GROUPS)]
    for _ in range(TOP_K):
        m = functools.reduce(jnp.maximum, [jnp.max(v, axis=0, keepdims=True) for v in vals])
        cand = [jnp.min(jnp.where(v == m, e, float(N_EXPERTS)), axis=0, keepdims=True)
                for v, e in zip(vals, eidx)]
        idx = functools.reduce(jnp.minimum, cand)
        for g in range(N_EXPERT_GROUPS):
            pick = eidx[g] == idx
            chosen[g] = jnp.where(pick, 1.0, chosen[g])
            vals[g] = jnp.where(pick, neg, vals[g])

    w = [jnp.where(c > 0.0, s, 0.0) for c, s in zip(chosen, sc_tiles)]
    denom = functools.reduce(jnp.add, [jnp.sum(t, axis=0, keepdims=True) for t in w])
    return jnp.concatenate([t / denom * ROUTED_SCALE for t in w], axis=0)


def _mix_kernel(x_ref, osb_ref, sga_ref, mb_ref, mod_ref, wsb_ref, wout_ref, g2_ref, wr_ref,
                rb_ref, wsgu_ref, wsd_ref, h2_ref, base_ref, gates_ref):
    m = mod_ref[0]
    gate1, shift2, scale2, gate2 = m[2:3], m[3:4], m[4:5], m[5:6]
    rows = x_ref.shape[0] // MIX_ROW_SPLITS
    for c in range(MIX_ROW_SPLITS):
        rs = slice(c * rows, (c + 1) * rows)
        ysb = _dot(osb_ref[rs, :], wsb_ref[...])
        merged = sga_ref[rs, :] * ysb.astype(BF16) + mb_ref[rs, :]
        x1 = x_ref[rs, :] + gate1 * _dot(merged, wout_ref[...])

        r = lax.rsqrt(jnp.mean(x1 * x1, axis=-1, keepdims=True) + NORM_EPS)
        h2 = ((x1 * r) * g2_ref[...]) * (1.0 + scale2) + shift2
        h_hi, h_lo = _split_bf16(h2)
        h2_ref[rs, :] = h_hi

        su = _dot(h_hi, wsgu_ref[...])
        sh = _dot((_silu(su[:, :SHARED_DIM]) * su[:, SHARED_DIM:]).astype(BF16), wsd_ref[...])
        base_ref[rs, :] = x1 + gate2 * sh

        hi_both = _dot(h_hi, wr_ref[...])
        logits = (hi_both[:, :LANES] + hi_both[:, LANES:]) + _dot(h_lo, wr_ref[:, :LANES])
        scores = jax.nn.sigmoid(logits.T[:N_EXPERTS, :])
        gates = _route(scores + rb_ref[...], scores)
        gates_pad = jnp.concatenate([gates, jnp.zeros((LANES - N_EXPERTS, rows), F32)], axis=0)
        gates_ref[rs, :] = gates_pad.T


def _mix(x2, osb, sga, mb, mod3, wsb, wout, g2, wr, rb, wsgu, wsd, seq):
    tm = TM_MIX
    tok = x2.shape[0]
    per_batch = seq // tm
    row = lambda i: (i, 0)
    const2 = lambda i: (0, 0)
    return pl.pallas_call(
        _mix_kernel,
        out_shape=(jax.ShapeDtypeStruct((tok, D_MODEL), BF16),
                   jax.ShapeDtypeStruct((tok, D_MODEL), F32),
                   jax.ShapeDtypeStruct((tok, LANES), F32)),
        grid=(tok // tm,),
        in_specs=[
            pl.BlockSpec((tm, D_MODEL), row),
            pl.BlockSpec((tm, SB_WIDTH), row),
            pl.BlockSpec((tm, D_MODEL), row),
            pl.BlockSpec((tm, D_MODEL), row),
            pl.BlockSpec((1, 6, D_MODEL), lambda i: (i // per_batch, 0, 0)),
            pl.BlockSpec(wsb.shape, const2),
            pl.BlockSpec(wout.shape, const2),
            pl.BlockSpec((1, D_MODEL), const2),
            pl.BlockSpec(wr.shape, const2),
            pl.BlockSpec(rb.shape, const2),
            pl.BlockSpec(wsgu.shape, const2),
            pl.BlockSpec(wsd.shape, const2),
        ],
        out_specs=(pl.BlockSpec((tm, D_MODEL), row),
                   pl.BlockSpec((tm, D_MODEL), row),
                   pl.BlockSpec((tm, LANES), row)),
        compiler_params=pltpu.CompilerParams(dimension_semantics=("arbitrary",),
                                             vmem_limit_bytes=VMEM_LIMIT),
        name="mix",
    )(x2, osb, sga, mb, mod3, wsb, wout, g2, wr, rb, wsgu, wsd)


def _moe_kernel(h_ref, g_ref, base_ref, mod_ref, wg_ref, wu_ref, wd_ref, o_ref, hm_ref):
    n_groups = N_EXPERTS // EXPERTS_PER_STEP
    s = pl.program_id(0)
    last = pl.num_programs(0) - 1
    group_up = jnp.minimum(s, last - 1) % n_groups
    group_dn = jnp.maximum(s - 1, 0) % n_groups
    width = EXPERTS_PER_STEP * EXPERT_DIM

    @pl.when(s == 0)
    def _():
        hm_ref[1] = jnp.zeros(hm_ref.shape[1:], BF16)

    @pl.when(group_dn == 0)
    def _():
        o_ref[...] = jnp.zeros_like(o_ref)

    wd = wd_ref[...].astype(BF16).reshape(width, D_MODEL)
    wgu = jnp.concatenate([wg_ref[j].astype(BF16) for j in range(EXPERTS_PER_STEP)]
                          + [wu_ref[j].astype(BF16) for j in range(EXPERTS_PER_STEP)], axis=1)
    rows = h_ref.shape[0] // MOE_ROW_SPLITS
    lane = lax.broadcasted_iota(jnp.int32, (rows, LANES), 1)
    rd, wr = (s + 1) % 2, s % 2
    for c in range(MOE_ROW_SPLITS):
        rs = slice(c * rows, (c + 1) * rows)
        o_ref[rs, :] += _dot(hm_ref[rd, rs, :], wd)
    for c in range(MOE_ROW_SPLITS):
        rs = slice(c * rows, (c + 1) * rows)
        up = _dot(h_ref[rs, :], wgu)
        gates = g_ref[rs, :]
        parts = []
        for j in range(EXPERTS_PER_STEP):
            a = up[:, j * EXPERT_DIM:(j + 1) * EXPERT_DIM]
            b = up[:, width + j * EXPERT_DIM:width + (j + 1) * EXPERT_DIM]
            gate = jnp.sum(jnp.where(lane == group_up * EXPERTS_PER_STEP + j, gates, 0.0),
                           axis=1, keepdims=True)
            parts.append((_silu(a) * b * gate).astype(BF16))
        hm_ref[wr, rs, :] = jnp.concatenate(parts, axis=1)

    @pl.when(jnp.logical_and(s > 0, group_dn == n_groups - 1))
    def _():
        gate2 = mod_ref[0][5:6]
        o_ref[...] = base_ref[...] + gate2 * o_ref[...]


def _moe(h2, gates, base, mod3, wg, wu, wd, seq):
    tm = TM_MOE
    tok = h2.shape[0]
    per_batch = seq // tm
    g = EXPERTS_PER_STEP
    n_groups = N_EXPERTS // g
    n_steps = (tok // tm) * n_groups + 1
    up_step = lambda s: jnp.minimum(s, n_steps - 2)
    dn_step = lambda s: jnp.maximum(s - 1, 0)
    up_tile = lambda s: (up_step(s) // n_groups, 0)
    dn_tile = lambda s: (dn_step(s) // n_groups, 0)
    return pl.pallas_call(
        _moe_kernel,
        out_shape=jax.ShapeDtypeStruct((tok, D_MODEL), F32),
        grid=(n_steps,),
        in_specs=[
            pl.BlockSpec((tm, D_MODEL), up_tile, pipeline_mode=pl.Buffered(1)),
            pl.BlockSpec((tm, LANES), up_tile, pipeline_mode=pl.Buffered(1)),
            pl.BlockSpec((tm, D_MODEL), dn_tile, pipeline_mode=pl.Buffered(1)),
            pl.BlockSpec((1, 6, D_MODEL), lambda s: (dn_step(s) // n_groups // per_batch, 0, 0)),
            pl.BlockSpec((g, D_MODEL, EXPERT_DIM), lambda s: (up_step(s) % n_groups, 0, 0)),
            pl.BlockSpec((g, D_MODEL, EXPERT_DIM), lambda s: (up_step(s) % n_groups, 0, 0)),
            pl.BlockSpec((g, EXPERT_DIM, D_MODEL), lambda s: (dn_step(s) % n_groups, 0, 0)),
        ],
        out_specs=pl.BlockSpec((tm, D_MODEL), dn_tile),
        scratch_shapes=[pltpu.VMEM((2, tm, g * EXPERT_DIM), BF16)],
        compiler_params=pltpu.CompilerParams(dimension_semantics=("arbitrary",),
                                             vmem_limit_bytes=VMEM_LIMIT),
        name="moe",
    )(h2, gates, base, mod3, wg, wu, wd)


def _block_diag(blocks):
    n = len(blocks)
    rows = [jnp.concatenate([blocks[i] if i == j else jnp.zeros_like(blocks[j]) for j in range(n)], axis=1)
            for i in range(n)]
    return jnp.concatenate(rows, axis=0)


def _layer(x2, c_pad, batch, seq, w_ada, b_ada, g_norm1, w_in, g_q, g_k, w_sb_out, w_pool_group,
           pool_scale, w_pool_out, w_out, g_norm2, w_router, router_bias,
           w_exp_gate, w_exp_up, w_exp_down, w_sh_gate, w_sh_up, w_sh_down):
    mod = _ada(c_pad, w_ada, b_ada[None, :])
    mod3 = mod[:batch].reshape(batch, 6, D_MODEL)

    heads_per_tile = MXU_TILE // HEAD_DIM
    head_avg = np.kron(np.eye(heads_per_tile, dtype=np.float32),
                       np.full((HEAD_DIM, HEAD_DIM), 1.0 / HEAD_DIM, np.float32))
    head_avg = jnp.asarray(head_avg, dtype=BF16)
    gq = (jnp.tile(g_q, SB_HEADS) * (HEAD_DIM ** -0.5 * LOG2_E))[None, :]
    gk = jnp.tile(g_k, SB_HEADS)[None, :]
    wpg = w_pool_group.astype(BF16)
    wpg_pairs = jnp.stack([_block_diag([wpg[2 * i], wpg[2 * i + 1]])
                           for i in range(len(POOL_WINDOWS) // 2)])
    q, k, v, sga, mb = _inproj(
        x2, mod3, g_norm1[None, :], w_in.astype(BF16), gq, gk, head_avg,
        wpg_pairs, pool_scale[None, :], w_pool_out.astype(BF16), batch, seq)

    shape3 = (batch, seq, SB_WIDTH)
    osb = _attention(q.reshape(shape3), k.reshape(shape3), v.reshape(shape3)).reshape(batch * seq, SB_WIDTH)

    wr = jnp.pad(w_router, ((0, 0), (0, LANES - N_EXPERTS)))
    wr_hi = wr.astype(BF16)
    wr_hi_lo = jnp.concatenate([wr_hi, (wr - wr_hi.astype(F32)).astype(BF16)], axis=1)
    wsgu = jnp.concatenate([w_sh_gate, w_sh_up], axis=1).astype(BF16)
    h2, base, gates = _mix(
        x2, osb, sga, mb, mod3, w_sb_out.astype(BF16), w_out.astype(BF16), g_norm2[None, :],
        wr_hi_lo, router_bias[:, None], wsgu, w_sh_down.astype(BF16), seq)

    return _moe(h2, gates, base, mod3, w_exp_gate, w_exp_up, w_exp_down, seq)


def kernel(x, c, w_ada, b_ada, g_norm1, w_in, g_q, g_k, w_sb_out, w_pool_group, pool_scale, w_pool_out,
           w_out, g_norm2, w_router, router_bias, w_exp_gate, w_exp_up, w_exp_down,
           w_sh_gate, w_sh_up, w_sh_down):
    batch, seq, d = x.shape
    x2 = x.reshape(batch * seq, d)
    c_pad = jnp.pad(c, ((0, 8 - batch), (0, 0)))
    for l in range(w_ada.shape[0]):
        x2 = _layer(x2, c_pad, batch, seq, w_ada[l], b_ada[l], g_norm1[l], w_in[l], g_q[l], g_k[l],
                    w_sb_out[l], w_pool_group[l], pool_scale[l], w_pool_out[l], w_out[l], g_norm2[l],
                    w_router[l], router_bias[l], w_exp_gate[l], w_exp_up[l], w_exp_down[l],
                    w_sh_gate[l], w_sh_up[l], w_sh_down[l])
    return x2.reshape(batch, seq, d)
```

```python
import functools

import jax
import jax.numpy as jnp
import numpy as np
from jax import lax
from jax.experimental import pallas as pl
from jax.experimental.pallas import tpu as pltpu

D_MODEL = 1024
SB_HEADS = 8
HEAD_DIM = 64
SB_WIDTH = SB_HEADS * HEAD_DIM
POOL_WINDOWS = (2, 4, 8, 16)
POOL_WIDTH = 512
POOL_GROUP_DIM = POOL_WIDTH // len(POOL_WINDOWS)
MAX_WIN = max(POOL_WINDOWS)
N_EXPERTS = 64
EXPERT_DIM = 128
TOP_K = 8
N_EXPERT_GROUPS = 8
TOPK_GROUPS = 4
SHARED_DIM = 256
ROUTED_SCALE = 2.5
NORM_EPS = 1e-6

LANES = 128
MXU_TILE = 256
VMEM_LIMIT = 56 * 1024 * 1024

TM_IN = 1024
TM_MIX = 1024
MIX_ROW_SPLITS = 4
IN_ROW_SPLITS = 4
TM_MOE = 2048
MOE_ROW_SPLITS = 4
EXPERTS_PER_STEP = 4
ATTN_Q = 128
ATTN_UNROLL = 2
ATTN_K = 384
ATTN_SKIP_LOG2 = 128.0
MASKED_LOGIT = -1e30
LOG2_E = 1.4426950408889634

F32 = jnp.float32
BF16 = jnp.bfloat16


def _dot(a, b):
    return jnp.dot(a, b, preferred_element_type=F32)


def _split_bf16(x):
    hi = x.astype(BF16)
    lo = (x - hi.astype(F32)).astype(BF16)
    return hi, lo


def _silu(x):
    return x * jax.nn.sigmoid(x)


def _ada_kernel(c_ref, w_ref, b_ref, o_ref):
    c = c_ref[...]
    o_ref[...] = _dot(_silu(c).astype(BF16), w_ref[...].astype(BF16)) + b_ref[...]


def _ada(c_pad, w_ada, b_ada):
    n = w_ada.shape[1]
    tn = D_MODEL
    return pl.pallas_call(
        _ada_kernel,
        out_shape=jax.ShapeDtypeStruct((c_pad.shape[0], n), F32),
        grid=(n // tn,),
        in_specs=[
            pl.BlockSpec(c_pad.shape, lambda i: (0, 0)),
            pl.BlockSpec((D_MODEL, tn), lambda i: (0, i)),
            pl.BlockSpec((1, tn), lambda i: (0, i)),
        ],
        out_specs=pl.BlockSpec((c_pad.shape[0], tn), lambda i: (0, i)),
        compiler_params=pltpu.CompilerParams(dimension_semantics=("arbitrary",),
                                             vmem_limit_bytes=VMEM_LIMIT),
        name="ada",
    )(c_pad, w_ada, b_ada)


def _in_kernel(x_ref, mod_ref, g1_ref, w_ref, gq_ref, gk_ref, bd_ref, wpg_ref, ps_ref, wpo_ref,
               q_ref, k_ref, v_ref, sga_ref, mb_ref, ubuf):
    j = pl.program_id(1)
    tm = x_ref.shape[0]
    m = mod_ref[0]
    shift1, scale1 = m[0:1], m[1:2]
    gain1 = g1_ref[...] * (1.0 + scale1)

    def headnorm(t, g_ref):
        sq = (t * t).astype(BF16)
        half = bd_ref.shape[1]
        ms = jnp.concatenate([_dot(sq[:, c:c + half], bd_ref[...]) for c in range(0, SB_WIDTH, half)], axis=1)
        return (t * lax.rsqrt(ms + NORM_EPS)) * g_ref[...]

    @pl.when(j == 0)
    def _():
        ubuf[0:MAX_WIN, :] = jnp.zeros((MAX_WIN, POOL_WIDTH), F32)

    @pl.when(j > 0)
    def _():
        ubuf[0:MAX_WIN, :] = ubuf[tm:tm + MAX_WIN, :]

    rows = tm // IN_ROW_SPLITS
    u0 = 3 * SB_WIDTH
    g0 = u0 + POOL_WIDTH
    for c in range(IN_ROW_SPLITS):
        r0 = c * rows
        rs = slice(r0, r0 + rows)
        x = x_ref[rs, :]
        r = lax.rsqrt(jnp.mean(x * x, axis=-1, keepdims=True) + NORM_EPS)
        hb = ((x * r) * gain1 + shift1).astype(BF16)

        def proj(lo, hi):
            return _dot(hb, w_ref[:, lo:hi])

        q_ref[rs, :] = headnorm(proj(0, SB_WIDTH), gq_ref).astype(BF16)
        k_ref[rs, :] = headnorm(proj(SB_WIDTH, 2 * SB_WIDTH), gk_ref).astype(BF16)
        v_ref[rs, :] = proj(2 * SB_WIDTH, 3 * SB_WIDTH).astype(BF16)

        u = proj(u0, u0 + POOL_WIDTH)
        ubuf[MAX_WIN + r0:MAX_WIN + r0 + rows, :] = u
        t = j * tm + r0 + lax.broadcasted_iota(jnp.int32, (rows, POOL_GROUP_DIM), 0)
        mixed = []
        for gi, w in enumerate(POOL_WINDOWS):
            lo = gi * POOL_GROUP_DIM
            ug = u[:, lo:lo + POOL_GROUP_DIM]
            s = ug
            for i in range(1, w):
                s = s + ubuf[MAX_WIN + r0 - i:MAX_WIN + r0 - i + rows, lo:lo + POOL_GROUP_DIM]
            cnt = jnp.minimum(t + 1, w).astype(F32)
            mixed.append((s / cnt - ug).astype(BF16))
        parts = [_dot(jnp.concatenate(mixed[2 * i:2 * i + 2], axis=1), wpg_ref[i])
                 for i in range(len(POOL_WINDOWS) // 2)]
        pooled = jnp.concatenate(parts, axis=1) * ps_ref[...]
        ypool = _dot(pooled.astype(BF16), wpo_ref[...])

        sga_ref[rs, :] = jax.nn.sigmoid(proj(g0, g0 + D_MODEL)).astype(BF16)
        mb_ref[rs, :] = (jax.nn.sigmoid(proj(g0 + D_MODEL, g0 + 2 * D_MODEL)) * ypool).astype(BF16)


def _inproj(x2, mod3, g1, w_in, gq, gk, bd, wpg, ps, wpo, batch, seq):
    tm = TM_IN
    nj = seq // tm
    tok = x2.shape[0]
    row = lambda b, j: (b * nj + j, 0)
    const2 = lambda b, j: (0, 0)
    out_sd = lambda n: jax.ShapeDtypeStruct((tok, n), BF16)
    return pl.pallas_call(
        _in_kernel,
        out_shape=(out_sd(SB_WIDTH), out_sd(SB_WIDTH), out_sd(SB_WIDTH), out_sd(D_MODEL), out_sd(D_MODEL)),
        grid=(batch, nj),
        in_specs=[
            pl.BlockSpec((tm, D_MODEL), row),
            pl.BlockSpec((1, 6, D_MODEL), lambda b, j: (b, 0, 0)),
            pl.BlockSpec((1, D_MODEL), const2),
            pl.BlockSpec(w_in.shape, const2),
            pl.BlockSpec((1, SB_WIDTH), const2),
            pl.BlockSpec((1, SB_WIDTH), const2),
            pl.BlockSpec(bd.shape, const2),
            pl.BlockSpec(wpg.shape, lambda b, j: (0, 0, 0)),
            pl.BlockSpec((1, POOL_WIDTH), const2),
            pl.BlockSpec((POOL_WIDTH, D_MODEL), const2),
        ],
        out_specs=(
            pl.BlockSpec((tm, SB_WIDTH), row),
            pl.BlockSpec((tm, SB_WIDTH), row),
            pl.BlockSpec((tm, SB_WIDTH), row),
            pl.BlockSpec((tm, D_MODEL), row),
            pl.BlockSpec((tm, D_MODEL), row),
        ),
        scratch_shapes=[pltpu.VMEM((MAX_WIN + tm, POOL_WIDTH), F32)],
        compiler_params=pltpu.CompilerParams(dimension_semantics=("arbitrary", "arbitrary"),
                                             vmem_limit_bytes=VMEM_LIMIT),
        name="inproj",
    )(x2, mod3, g1, w_in, gq, gk, bd, wpg, ps, wpo)


def _attn_kernel(q_ref, k_ref, v_ref, o_ref, tri_ref):
    tq, tk = ATTN_Q, ATTN_K
    seq = q_ref.shape[1]
    n_pairs = q_ref.shape[2] // LANES
    row = lax.broadcasted_iota(jnp.int32, (tk, tk), 0)
    col = lax.broadcasted_iota(jnp.int32, (tk, tk), 1)
    tri_ref[...] = jnp.where(row >= col, 1.0, 0.0).astype(BF16)
    kcol = lax.broadcasted_iota(jnp.int32, (tq, tk), 1)
    col_minus_row = kcol - lax.broadcasted_iota(jnp.int32, (tq, tk), 0)
    first_head = lax.broadcasted_iota(jnp.int32, (tq, LANES), 1) < HEAD_DIM
    first_head_k = lax.broadcasted_iota(jnp.int32, (tk, LANES), 1) < HEAD_DIM

    n_heads = 2 * n_pairs

    def window(q_pairs, start, mask, carry=None, accs=None):
        zs = []
        for p in range(n_pairs):
            kb = k_ref[0, pl.ds(start, tk), p * LANES:(p + 1) * LANES]
            zs.append(lax.dot_general(q_pairs[p], kb, (((1,), (1,)), ((), ())), preferred_element_type=F32))
        z = jnp.concatenate(zs, axis=0).reshape(n_heads, tq, tk)
        z = jnp.where(mask[None], z, MASKED_LOGIT)
        fail = jnp.maximum(z, 0.0) + jnp.log2(1.0 + jnp.exp2(-jnp.abs(z)))
        spent = _dot(fail.astype(BF16).reshape(n_heads * tq, tk), tri_ref[...]).reshape(n_heads, tq, tk)
        log_w = z - spent
        total = spent[:, :, 0:1]
        if carry is not None:
            log_w = log_w - carry
            total = total + carry
        a = jnp.exp2(log_w).astype(BF16)
        new_accs = []
        for p in range(n_pairs):
            vb = v_ref[0, pl.ds(start, tk), p * LANES:(p + 1) * LANES]
            zero = jnp.zeros_like(vb)
            v_diag = jnp.concatenate([jnp.where(first_head_k, vb, zero), jnp.where(first_head_k, zero, vb)],
                                     axis=0)
            av = _dot(jnp.concatenate([a[2 * p], a[2 * p + 1]], axis=-1), v_diag)
            new_accs.append(av if accs is None else accs[p] + av)
        return total, tuple(new_accs)

    def q_body(it, _):
        blocks = []
        for u in range(ATTN_UNROLL):
            qs = pl.multiple_of((it * ATTN_UNROLL + u) * tq, tq)
            ws = pl.multiple_of(jnp.maximum(qs - (tk - tq), 0), tq)
            causal = col_minus_row < (qs - ws)
            q_pairs = []
            for p in range(n_pairs):
                q2 = q_ref[0, pl.ds(qs, tq), p * LANES:(p + 1) * LANES]
                zero = jnp.zeros_like(q2)
                q_pairs.append(jnp.concatenate([jnp.where(first_head, q2, zero),
                                                jnp.where(first_head, zero, q2)], axis=0))
            carry, accs = window(q_pairs, ws, causal)
            blocks.append((qs, ws, q_pairs, carry, accs))

        for qs, ws, q_pairs, carry, accs in blocks:
            def cond(st):
                start, carry, _ = st
                return jnp.logical_and(start > 0, jnp.min(carry) < ATTN_SKIP_LOG2)

            def body(st, q_pairs=q_pairs):
                start, carry, accs = st
                nxt = pl.multiple_of(jnp.maximum(start - tk, 0), tq)
                carry, accs = window(q_pairs, nxt, kcol < (start - nxt), carry, accs)
                return nxt, carry, accs

            _, _, accs = lax.while_loop(cond, body, (ws, carry, accs))
            o_ref[0, pl.ds(qs, tq), :] = jnp.concatenate(accs, axis=1).astype(BF16)
        return 0

    lax.fori_loop(0, seq // (tq * ATTN_UNROLL), q_body, 0)


def _attention(q3, k3, v3):
    batch, seq, width = q3.shape
    spec = pl.BlockSpec((1, seq, width), lambda b: (b, 0, 0))
    return pl.pallas_call(
        _attn_kernel,
        out_shape=jax.ShapeDtypeStruct((batch, seq, width), BF16),
        grid=(batch,),
        in_specs=[spec, spec, spec],
        out_specs=spec,
        scratch_shapes=[pltpu.VMEM((ATTN_K, ATTN_K), BF16)],
        compiler_params=pltpu.CompilerParams(dimension_semantics=("arbitrary",),
                                             vmem_limit_bytes=VMEM_LIMIT),
        name="attn",
    )(q3, k3, v3)


def _route(sel, scores):
    per_group = N_EXPERTS // N_EXPERT_GROUPS
    tm = sel.shape[1]
    neg = -jnp.inf
    tiles = [sel[per_group * g:per_group * (g + 1), :] for g in range(N_EXPERT_GROUPS)]
    sc_tiles = [scores[per_group * g:per_group * (g + 1), :] for g in range(N_EXPERT_GROUPS)]

    grp = []
    for tg in tiles:
        m1 = jnp.max(tg, axis=0, keepdims=True)
        eq = tg == m1
        n_eq = jnp.sum(jnp.where(eq, 1.0, 0.0), axis=0, keepdims=True)
        second = jnp.max(jnp.where(eq, neg, tg), axis=0, keepdims=True)
        grp.append(m1 + jnp.where(n_eq >= 2.0, m1, second))

    gtile = jnp.concatenate(grp, axis=0)
    gidx = lax.broadcasted_iota(jnp.int32, (N_EXPERT_GROUPS, tm), 0)
    rank = jnp.zeros((N_EXPERT_GROUPS, tm), F32)
    for gp in range(N_EXPERT_GROUPS):
        other = jnp.broadcast_to(grp[gp], (N_EXPERT_GROUPS, tm))
        rank = rank + jnp.where(other > gtile, 1.0, 0.0)
        rank = rank + jnp.where(jnp.logical_and(other == gtile, gidx > gp), 1.0, 0.0)
    keep = jnp.where(rank < float(TOPK_GROUPS), 1.0, 0.0)
    vals = [jnp.where(jnp.broadcast_to(keep[g:g + 1, :], (per_group, tm)) > 0.0, tiles[g], neg)
            for g in range(N_EXPERT_GROUPS)]

    sub = lax.broadcasted_iota(jnp.int32, (per_group, tm), 0).astype(F32)
    eidx = [sub + float(per_group * g) for g in range(N_EXPERT_GROUPS)]
    chosen = [jnp.zeros((per_group, tm), F32) for _ in range(N_EXPERT_GROUPS)]
    for _ in range(TOP_K):
        m = jnp.max(functools.reduce(jnp.maximum, vals), axis=0, keepdims=True)
        cand = [jnp.where(v == m, e, float(N_EXPERTS)) for v, e in zip(vals, eidx)]
        idx = jnp.min(functools.reduce(jnp.minimum, cand), axis=0, keepdims=True)
        for g in range(N_EXPERT_GROUPS):
            pick = eidx[g] == idx
            chosen[g] = jnp.where(pick, 1.0, chosen[g])
            vals[g] = jnp.where(pick, neg, vals[g])

    w = [jnp.where(c > 0.0, s, 0.0) for c, s in zip(chosen, sc_tiles)]
    denom = functools.reduce(jnp.add, [jnp.sum(t, axis=0, keepdims=True) for t in w])
    return jnp.concatenate([t / denom * ROUTED_SCALE for t in w], axis=0)


def _mix_kernel(x_ref, osb_ref, sga_ref, mb_ref, mod_ref, wsb_ref, wout_ref, g2_ref, wr_ref,
                rb_ref, wsgu_ref, wsd_ref, h2_ref, base_ref, gates_ref):
    m = mod_ref[0]
    gate1, shift2, scale2, gate2 = m[2:3], m[3:4], m[4:5], m[5:6]
    gain2 = g2_ref[...] * (1.0 + scale2)
    rows = x_ref.shape[0] // MIX_ROW_SPLITS
    for c in range(MIX_ROW_SPLITS):
        rs = slice(c * rows, (c + 1) * rows)
        ysb = _dot(osb_ref[rs, :], wsb_ref[...])
        merged = sga_ref[rs, :] * ysb.astype(BF16) + mb_ref[rs, :]
        x1 = x_ref[rs, :] + gate1 * _dot(merged, wout_ref[...])

        r = lax.rsqrt(jnp.mean(x1 * x1, axis=-1, keepdims=True) + NORM_EPS)
        h2 = (x1 * r) * gain2 + shift2
        h_hi, h_lo = _split_bf16(h2)
        h2_ref[rs, :] = h_hi

        su = _dot(h_hi, wsgu_ref[...])
        sh = _dot((_silu(su[:, :SHARED_DIM]) * su[:, SHARED_DIM:]).astype(BF16), wsd_ref[...])
        base_ref[rs, :] = x1 + gate2 * sh

        hi_both = _dot(h_hi, wr_ref[...])
        logits = (hi_both[:, :LANES] + hi_both[:, LANES:]) + _dot(h_lo, wr_ref[:, :LANES])
        scores = jax.nn.sigmoid(logits.T[:N_EXPERTS, :])
        gates = _route(scores + rb_ref[...], scores)
        gates_pad = jnp.concatenate([gates, jnp.zeros((LANES - N_EXPERTS, rows), F32)], axis=0)
        gates_ref[rs, :] = gates_pad.T


def _mix(x2, osb, sga, mb, mod3, wsb, wout, g2, wr, rb, wsgu, wsd, seq):
    tm = TM_MIX
    tok = x2.shape[0]
    per_batch = seq // tm
    row = lambda i: (i, 0)
    const2 = lambda i: (0, 0)
    return pl.pallas_call(
        _mix_kernel,
        out_shape=(jax.ShapeDtypeStruct((tok, D_MODEL), BF16),
                   jax.ShapeDtypeStruct((tok, D_MODEL), F32),
                   jax.ShapeDtypeStruct((tok, LANES), F32)),
        grid=(tok // tm,),
        in_specs=[
            pl.BlockSpec((tm, D_MODEL), row),
            pl.BlockSpec((tm, SB_WIDTH), row),
            pl.BlockSpec((tm, D_MODEL), row),
            pl.BlockSpec((tm, D_MODEL), row),
            pl.BlockSpec((1, 6, D_MODEL), lambda i: (i // per_batch, 0, 0)),
            pl.BlockSpec(wsb.shape, const2),
            pl.BlockSpec(wout.shape, const2),
            pl.BlockSpec((1, D_MODEL), const2),
            pl.BlockSpec(wr.shape, const2),
            pl.BlockSpec(rb.shape, const2),
            pl.BlockSpec(wsgu.shape, const2),
            pl.BlockSpec(wsd.shape, const2),
        ],
        out_specs=(pl.BlockSpec((tm, D_MODEL), row),
                   pl.BlockSpec((tm, D_MODEL), row),
                   pl.BlockSpec((tm, LANES), row)),
        compiler_params=pltpu.CompilerParams(dimension_semantics=("arbitrary",),
                                             vmem_limit_bytes=VMEM_LIMIT),
        name="mix",
    )(x2, osb, sga, mb, mod3, wsb, wout, g2, wr, rb, wsgu, wsd)


def _moe_kernel(h_ref, g_ref, base_ref, mod_ref, wg_ref, wu_ref, wd_ref, o_ref, hm_ref):
    n_groups = N_EXPERTS // EXPERTS_PER_STEP
    s = pl.program_id(0)
    last = pl.num_programs(0) - 1
    group_up = jnp.minimum(s, last - 1) % n_groups
    group_dn = jnp.maximum(s - 1, 0) % n_groups
    width = EXPERTS_PER_STEP * EXPERT_DIM

    @pl.when(s == 0)
    def _():
        hm_ref[1] = jnp.zeros(hm_ref.shape[1:], BF16)

    @pl.when(group_dn == 0)
    def _():
        o_ref[...] = jnp.zeros_like(o_ref)

    wd = wd_ref[...].astype(BF16).reshape(width, D_MODEL)
    wgu = jnp.concatenate([wg_ref[j].astype(BF16) for j in range(EXPERTS_PER_STEP)]
                          + [wu_ref[j].astype(BF16) for j in range(EXPERTS_PER_STEP)], axis=1)
    rows = h_ref.shape[0] // MOE_ROW_SPLITS
    lane = lax.broadcasted_iota(jnp.int32, (rows, LANES), 1)
    rd, wr = (s + 1) % 2, s % 2
    for c in range(MOE_ROW_SPLITS):
        rs = slice(c * rows, (c + 1) * rows)
        o_ref[rs, :] += _dot(hm_ref[rd, rs, :], wd)
    for c in range(MOE_ROW_SPLITS):
        rs = slice(c * rows, (c + 1) * rows)
        up = _dot(h_ref[rs, :], wgu)
        gates = g_ref[rs, :]
        parts = []
        for j in range(EXPERTS_PER_STEP):
            a = up[:, j * EXPERT_DIM:(j + 1) * EXPERT_DIM]
            b = up[:, width + j * EXPERT_DIM:width + (j + 1) * EXPERT_DIM]
            gate = jnp.sum(jnp.where(lane == group_up * EXPERTS_PER_STEP + j, gates, 0.0),
                           axis=1, keepdims=True)
            parts.append((_silu(a) * b * gate).astype(BF16))
        hm_ref[wr, rs, :] = jnp.concatenate(parts, axis=1)

    @pl.when(jnp.logical_and(s > 0, group_dn == n_groups - 1))
    def _():
        gate2 = mod_ref[0][5:6]
        o_ref[...] = base_ref[...] + gate2 * o_ref[...]


def _moe(h2, gates, base, mod3, wg, wu, wd, seq):
    tm = TM_MOE
    tok = h2.shape[0]
    per_batch = seq // tm
    g = EXPERTS_PER_STEP
    n_groups = N_EXPERTS // g
    n_steps = (tok // tm) * n_groups + 1
    up_step = lambda s: jnp.minimum(s, n_steps - 2)
    dn_step = lambda s: jnp.maximum(s - 1, 0)
    up_tile = lambda s: (up_step(s) // n_groups, 0)
    dn_tile = lambda s: (dn_step(s) // n_groups, 0)
    return pl.pallas_call(
        _moe_kernel,
        out_shape=jax.ShapeDtypeStruct((tok, D_MODEL), F32),
        grid=(n_steps,),
        in_specs=[
            pl.BlockSpec((tm, D_MODEL), up_tile, pipeline_mode=pl.Buffered(1)),
            pl.BlockSpec((tm, LANES), up_tile, pipeline_mode=pl.Buffered(1)),
            pl.BlockSpec((tm, D_MODEL), dn_tile, pipeline_mode=pl.Buffered(1)),
            pl.BlockSpec((1, 6, D_MODEL), lambda s: (dn_step(s) // n_groups // per_batch, 0, 0)),
            pl.BlockSpec((g, D_MODEL, EXPERT_DIM), lambda s: (up_step(s) % n_groups, 0, 0)),
            pl.BlockSpec((g, D_MODEL, EXPERT_DIM), lambda s: (up_step(s) % n_groups, 0, 0)),
            pl.BlockSpec((g, EXPERT_DIM, D_MODEL), lambda s: (dn_step(s) % n_groups, 0, 0)),
        ],
        out_specs=pl.BlockSpec((tm, D_MODEL), dn_tile),
        scratch_shapes=[pltpu.VMEM((2, tm, g * EXPERT_DIM), BF16)],
        compiler_params=pltpu.CompilerParams(dimension_semantics=("arbitrary",),
                                             vmem_limit_bytes=VMEM_LIMIT),
        name="moe",
    )(h2, gates, base, mod3, wg, wu, wd)


def _block_diag(blocks):
    n = len(blocks)
    rows = [jnp.concatenate([blocks[i] if i == j else jnp.zeros_like(blocks[j]) for j in range(n)], axis=1)
            for i in range(n)]
    return jnp.concatenate(rows, axis=0)


def _layer(x2, c_pad, batch, seq, w_ada, b_ada, g_norm1, w_in, g_q, g_k, w_sb_out, w_pool_group,
           pool_scale, w_pool_out, w_out, g_norm2, w_router, router_bias,
           w_exp_gate, w_exp_up, w_exp_down, w_sh_gate, w_sh_up, w_sh_down):
    mod = _ada(c_pad, w_ada, b_ada[None, :])
    mod3 = mod[:batch].reshape(batch, 6, D_MODEL)

    heads_per_tile = MXU_TILE // HEAD_DIM
    head_avg = np.kron(np.eye(heads_per_tile, dtype=np.float32),
                       np.full((HEAD_DIM, HEAD_DIM), 1.0 / HEAD_DIM, np.float32))
    head_avg = jnp.asarray(head_avg, dtype=BF16)
    gq = (jnp.tile(g_q, SB_HEADS) * (HEAD_DIM ** -0.5 * LOG2_E))[None, :]
    gk = jnp.tile(g_k, SB_HEADS)[None, :]
    wpg = w_pool_group.astype(BF16)
    wpg_pairs = jnp.stack([_block_diag([wpg[2 * i], wpg[2 * i + 1]])
                           for i in range(len(POOL_WINDOWS) // 2)])
    q, k, v, sga, mb = _inproj(
        x2, mod3, g_norm1[None, :], w_in.astype(BF16), gq, gk, head_avg,
        wpg_pairs, pool_scale[None, :], w_pool_out.astype(BF16), batch, seq)

    shape3 = (batch, seq, SB_WIDTH)
    osb = _attention(q.reshape(shape3), k.reshape(shape3), v.reshape(shape3)).reshape(batch * seq, SB_WIDTH)

    wr = jnp.pad(w_router, ((0, 0), (0, LANES - N_EXPERTS)))
    wr_hi = wr.astype(BF16)
    wr_hi_lo = jnp.concatenate([wr_hi, (wr - wr_hi.astype(F32)).astype(BF16)], axis=1)
    wsgu = jnp.concatenate([w_sh_gate, w_sh_up], axis=1).astype(BF16)
    h2, base, gates = _mix(
        x2, osb, sga, mb, mod3, w_sb_out.astype(BF16), w_out.astype(BF16), g_norm2[None, :],
        wr_hi_lo, router_bias[:, None], wsgu, w_sh_down.astype(BF16), seq)

    return _moe(h2, gates, base, mod3, w_exp_gate, w_exp_up, w_exp_down, seq)


def kernel(x, c, w_ada, b_ada, g_norm1, w_in, g_q, g_k, w_sb_out, w_pool_group, pool_scale, w_pool_out,
           w_out, g_norm2, w_router, router_bias, w_exp_gate, w_exp_up, w_exp_down,
           w_sh_gate, w_sh_up, w_sh_down):
    batch, seq, d = x.shape
    x2 = x.reshape(batch * seq, d)
    c_pad = jnp.pad(c, ((0, 8 - batch), (0, 0)))
    for l in range(w_ada.shape[0]):
        x2 = _layer(x2, c_pad, batch, seq, w_ada[l], b_ada[l], g_norm1[l], w_in[l], g_q[l], g_k[l],
                    w_sb_out[l], w_pool_group[l], pool_scale[l], w_pool_out[l], w_out[l], g_norm2[l],
                    w_router[l], router_bias[l], w_exp_gate[l], w_exp_up[l], w_exp_down[l],
                    w_sh_gate[l], w_sh_up[l], w_sh_down[l])
    return x2.reshape(batch, seq, d)
```

```python
import functools

import jax
import jax.numpy as jnp
import numpy as np
from jax import lax
from jax.experimental import pallas as pl
from jax.experimental.pallas import tpu as pltpu

D_MODEL = 1024
SB_HEADS = 8
HEAD_DIM = 64
SB_WIDTH = SB_HEADS * HEAD_DIM
POOL_WINDOWS = (2, 4, 8, 16)
POOL_WIDTH = 512
POOL_GROUP_DIM = POOL_WIDTH // len(POOL_WINDOWS)
MAX_WIN = max(POOL_WINDOWS)
N_EXPERTS = 64
EXPERT_DIM = 128
TOP_K = 8
N_EXPERT_GROUPS = 8
TOPK_GROUPS = 4
SHARED_DIM = 256
ROUTED_SCALE = 2.5
NORM_EPS = 1e-6

LANES = 128
MXU_TILE = 256
VMEM_LIMIT = 56 * 1024 * 1024

TM_IN = 1024
TM_MIX = 1024
MIX_ROW_SPLITS = 4
IN_ROW_SPLITS = 4
TM_MOE = 2048
MOE_ROW_SPLITS = 4
EXPERTS_PER_STEP = 4
ATTN_Q = 128
ATTN_UNROLL = 2
ATTN_K = 384
ATTN_SKIP_LOG2 = 128.0
MASKED_LOGIT = -1e30
LOG2_E = 1.4426950408889634

F32 = jnp.float32
BF16 = jnp.bfloat16


def _dot(a, b):
    return jnp.dot(a, b, preferred_element_type=F32)


def _split_bf16(x):
    hi = x.astype(BF16)
    lo = (x - hi.astype(F32)).astype(BF16)
    return hi, lo


def _silu(x):
    return x * jax.nn.sigmoid(x)


def _ada_kernel(c_ref, w_ref, b_ref, o_ref):
    c = c_ref[...]
    o_ref[...] = _dot(_silu(c).astype(BF16), w_ref[...].astype(BF16)) + b_ref[...]


def _ada(c_pad, w_ada, b_ada):
    n = w_ada.shape[1]
    tn = D_MODEL
    return pl.pallas_call(
        _ada_kernel,
        out_shape=jax.ShapeDtypeStruct((c_pad.shape[0], n), F32),
        grid=(n // tn,),
        in_specs=[
            pl.BlockSpec(c_pad.shape, lambda i: (0, 0)),
            pl.BlockSpec((D_MODEL, tn), lambda i: (0, i)),
            pl.BlockSpec((1, tn), lambda i: (0, i)),
        ],
        out_specs=pl.BlockSpec((c_pad.shape[0], tn), lambda i: (0, i)),
        compiler_params=pltpu.CompilerParams(dimension_semantics=("arbitrary",),
                                             vmem_limit_bytes=VMEM_LIMIT),
        name="ada",
    )(c_pad, w_ada, b_ada)


def _in_kernel(x_ref, mod_ref, g1_ref, w_ref, gq_ref, gk_ref, bd_ref, wpg_ref, ps_ref, wpo_ref,
               q_ref, k_ref, v_ref, sga_ref, mb_ref, ubuf, w_bf, wpo_bf):
    j = pl.program_id(1)
    tm = x_ref.shape[0]

    @pl.when(jnp.logical_and(pl.program_id(0) == 0, j == 0))
    def _():
        w_bf[...] = w_ref[...].astype(BF16)
        wpo_bf[...] = wpo_ref[...].astype(BF16)
    m = mod_ref[0]
    shift1, scale1 = m[0:1], m[1:2]
    gain1 = g1_ref[...] * (1.0 + scale1)

    def headnorm(t, g_ref):
        sq = (t * t).astype(BF16)
        half = bd_ref.shape[1]
        ms = jnp.concatenate([_dot(sq[:, c:c + half], bd_ref[...]) for c in range(0, SB_WIDTH, half)], axis=1)
        return (t * lax.rsqrt(ms + NORM_EPS)) * g_ref[...]

    @pl.when(j == 0)
    def _():
        ubuf[0:MAX_WIN, :] = jnp.zeros((MAX_WIN, POOL_WIDTH), F32)

    @pl.when(j > 0)
    def _():
        ubuf[0:MAX_WIN, :] = ubuf[tm:tm + MAX_WIN, :]

    rows = tm // IN_ROW_SPLITS
    u0 = 3 * SB_WIDTH
    g0 = u0 + POOL_WIDTH
    for c in range(IN_ROW_SPLITS):
        r0 = c * rows
        rs = slice(r0, r0 + rows)
        x = x_ref[rs, :]
        r = lax.rsqrt(jnp.mean(x * x, axis=-1, keepdims=True) + NORM_EPS)
        hb = ((x * r) * gain1 + shift1).astype(BF16)

        def proj(lo, hi):
            return _dot(hb, w_bf[:, lo:hi])

        q_ref[rs, :] = headnorm(proj(0, SB_WIDTH), gq_ref).astype(BF16)
        k_ref[rs, :] = headnorm(proj(SB_WIDTH, 2 * SB_WIDTH), gk_ref).astype(BF16)
        v_ref[rs, :] = proj(2 * SB_WIDTH, 3 * SB_WIDTH).astype(BF16)

        u = proj(u0, u0 + POOL_WIDTH)
        ubuf[MAX_WIN + r0:MAX_WIN + r0 + rows, :] = u
        t = j * tm + r0 + lax.broadcasted_iota(jnp.int32, (rows, POOL_GROUP_DIM), 0)
        mixed = []
        for gi, w in enumerate(POOL_WINDOWS):
            lo = gi * POOL_GROUP_DIM
            ug = u[:, lo:lo + POOL_GROUP_DIM]
            s = ug
            for i in range(1, w):
                s = s + ubuf[MAX_WIN + r0 - i:MAX_WIN + r0 - i + rows, lo:lo + POOL_GROUP_DIM]
            cnt = jnp.minimum(t + 1, w).astype(F32)
            mixed.append((s / cnt - ug).astype(BF16))
        parts = [_dot(jnp.concatenate(mixed[2 * i:2 * i + 2], axis=1), wpg_ref[i])
                 for i in range(len(POOL_WINDOWS) // 2)]
        pooled = jnp.concatenate(parts, axis=1) * ps_ref[...]
        ypool = _dot(pooled.astype(BF16), wpo_bf[...])

        sga_ref[rs, :] = jax.nn.sigmoid(proj(g0, g0 + D_MODEL)).astype(BF16)
        mb_ref[rs, :] = (jax.nn.sigmoid(proj(g0 + D_MODEL, g0 + 2 * D_MODEL)) * ypool).astype(BF16)


def _inproj(x2, mod3, g1, w_in, gq, gk, bd, wpg, ps, wpo, batch, seq):
    tm = TM_IN
    nj = seq // tm
    tok = x2.shape[0]
    row = lambda b, j: (b * nj + j, 0)
    const2 = lambda b, j: (0, 0)
    out_sd = lambda n: jax.ShapeDtypeStruct((tok, n), BF16)
    return pl.pallas_call(
        _in_kernel,
        out_shape=(out_sd(SB_WIDTH), out_sd(SB_WIDTH), out_sd(SB_WIDTH), out_sd(D_MODEL), out_sd(D_MODEL)),
        grid=(batch, nj),
        in_specs=[
            pl.BlockSpec((tm, D_MODEL), row),
            pl.BlockSpec((1, 6, D_MODEL), lambda b, j: (b, 0, 0)),
            pl.BlockSpec((1, D_MODEL), const2),
            pl.BlockSpec(w_in.shape, const2),
            pl.BlockSpec((1, SB_WIDTH), const2),
            pl.BlockSpec((1, SB_WIDTH), const2),
            pl.BlockSpec(bd.shape, const2),
            pl.BlockSpec(wpg.shape, lambda b, j: (0, 0, 0)),
            pl.BlockSpec((1, POOL_WIDTH), const2),
            pl.BlockSpec((POOL_WIDTH, D_MODEL), const2),
        ],
        out_specs=(
            pl.BlockSpec((tm, SB_WIDTH), row),
            pl.BlockSpec((tm, SB_WIDTH), row),
            pl.BlockSpec((tm, SB_WIDTH), row),
            pl.BlockSpec((tm, D_MODEL), row),
            pl.BlockSpec((tm, D_MODEL), row),
        ),
        scratch_shapes=[pltpu.VMEM((MAX_WIN + tm, POOL_WIDTH), F32),
                        pltpu.VMEM(w_in.shape, BF16),
                        pltpu.VMEM(wpo.shape, BF16)],
        compiler_params=pltpu.CompilerParams(dimension_semantics=("arbitrary", "arbitrary"),
                                             vmem_limit_bytes=VMEM_LIMIT),
        name="inproj",
    )(x2, mod3, g1, w_in, gq, gk, bd, wpg, ps, wpo)


def _attn_kernel(q_ref, k_ref, v_ref, o_ref, tri_ref):
    tq, tk = ATTN_Q, ATTN_K
    seq = q_ref.shape[1]
    n_pairs = q_ref.shape[2] // LANES
    row = lax.broadcasted_iota(jnp.int32, (tk, tk), 0)
    col = lax.broadcasted_iota(jnp.int32, (tk, tk), 1)
    tri_ref[...] = jnp.where(row >= col, 1.0, 0.0).astype(BF16)
    kcol = lax.broadcasted_iota(jnp.int32, (tq, tk), 1)
    col_minus_row = kcol - lax.broadcasted_iota(jnp.int32, (tq, tk), 0)
    first_head = lax.broadcasted_iota(jnp.int32, (tq, LANES), 1) < HEAD_DIM
    first_head_k = lax.broadcasted_iota(jnp.int32, (tk, LANES), 1) < HEAD_DIM

    n_heads = 2 * n_pairs

    def window(q_pairs, start, mask, carry=None, accs=None):
        zs = []
        for p in range(n_pairs):
            kb = k_ref[0, pl.ds(start, tk), p * LANES:(p + 1) * LANES]
            zs.append(lax.dot_general(q_pairs[p], kb, (((1,), (1,)), ((), ())), preferred_element_type=F32))
        z = jnp.concatenate(zs, axis=0).reshape(n_heads, tq, tk)
        z = jnp.where(mask[None], z, MASKED_LOGIT)
        fail = jnp.maximum(z, 0.0) + jnp.log2(1.0 + jnp.exp2(-jnp.abs(z)))
        spent = _dot(fail.astype(BF16).reshape(n_heads * tq, tk), tri_ref[...]).reshape(n_heads, tq, tk)
        log_w = z - spent
        total = spent[:, :, 0:1]
        if carry is not None:
            log_w = log_w - carry
            total = total + carry
        a = jnp.exp2(log_w).astype(BF16)
        new_accs = []
        for p in range(n_pairs):
            vb = v_ref[0, pl.ds(start, tk), p * LANES:(p + 1) * LANES]
            zero = jnp.zeros_like(vb)
            v_diag = jnp.concatenate([jnp.where(first_head_k, vb, zero), jnp.where(first_head_k, zero, vb)],
                                     axis=0)
            av = _dot(jnp.concatenate([a[2 * p], a[2 * p + 1]], axis=-1), v_diag)
            new_accs.append(av if accs is None else accs[p] + av)
        return total, tuple(new_accs)

    def q_body(it, _):
        blocks = []
        for u in range(ATTN_UNROLL):
            qs = pl.multiple_of((it * ATTN_UNROLL + u) * tq, tq)
            ws = pl.multiple_of(jnp.maximum(qs - (tk - tq), 0), tq)
            causal = col_minus_row < (qs - ws)
            q_pairs = []
            for p in range(n_pairs):
                q2 = q_ref[0, pl.ds(qs, tq), p * LANES:(p + 1) * LANES]
                zero = jnp.zeros_like(q2)
                q_pairs.append(jnp.concatenate([jnp.where(first_head, q2, zero),
                                                jnp.where(first_head, zero, q2)], axis=0))
            carry, accs = window(q_pairs, ws, causal)
            blocks.append((qs, ws, q_pairs, carry, accs))

        for qs, ws, q_pairs, carry, accs in blocks:
            def cond(st):
                start, carry, _ = st
                return jnp.logical_and(start > 0, jnp.min(carry) < ATTN_SKIP_LOG2)

            def body(st, q_pairs=q_pairs):
                start, carry, accs = st
                nxt = pl.multiple_of(jnp.maximum(start - tk, 0), tq)
                carry, accs = window(q_pairs, nxt, kcol < (start - nxt), carry, accs)
                return nxt, carry, accs

            _, _, accs = lax.while_loop(cond, body, (ws, carry, accs))
            o_ref[0, pl.ds(qs, tq), :] = jnp.concatenate(accs, axis=1).astype(BF16)
        return 0

    lax.fori_loop(0, seq // (tq * ATTN_UNROLL), q_body, 0)


def _attention(q3, k3, v3):
    batch, seq, width = q3.shape
    spec = pl.BlockSpec((1, seq, width), lambda b: (b, 0, 0))
    return pl.pallas_call(
        _attn_kernel,
        out_shape=jax.ShapeDtypeStruct((batch, seq, width), BF16),
        grid=(batch,),
        in_specs=[spec, spec, spec],
        out_specs=spec,
        scratch_shapes=[pltpu.VMEM((ATTN_K, ATTN_K), BF16)],
        compiler_params=pltpu.CompilerParams(dimension_semantics=("arbitrary",),
                                             vmem_limit_bytes=VMEM_LIMIT),
        name="attn",
    )(q3, k3, v3)


def _route(sel, scores):
    per_group = N_EXPERTS // N_EXPERT_GROUPS
    tm = sel.shape[1]
    neg = -jnp.inf
    tiles = [sel[per_group * g:per_group * (g + 1), :] for g in range(N_EXPERT_GROUPS)]
    sc_tiles = [scores[per_group * g:per_group * (g + 1), :] for g in range(N_EXPERT_GROUPS)]

    grp = []
    for tg in tiles:
        m1 = jnp.max(tg, axis=0, keepdims=True)
        eq = tg == m1
        n_eq = jnp.sum(jnp.where(eq, 1.0, 0.0), axis=0, keepdims=True)
        second = jnp.max(jnp.where(eq, neg, tg), axis=0, keepdims=True)
        grp.append(m1 + jnp.where(n_eq >= 2.0, m1, second))

    gtile = jnp.concatenate(grp, axis=0)
    gidx = lax.broadcasted_iota(jnp.int32, (N_EXPERT_GROUPS, tm), 0)
    rank = jnp.zeros((N_EXPERT_GROUPS, tm), F32)
    for gp in range(N_EXPERT_GROUPS):
        other = jnp.broadcast_to(grp[gp], (N_EXPERT_GROUPS, tm))
        rank = rank + jnp.where(other > gtile, 1.0, 0.0)
        rank = rank + jnp.where(jnp.logical_and(other == gtile, gidx > gp), 1.0, 0.0)
    keep = jnp.where(rank < float(TOPK_GROUPS), 1.0, 0.0)
    vals = [jnp.where(jnp.broadcast_to(keep[g:g + 1, :], (per_group, tm)) > 0.0, tiles[g], neg)
            for g in range(N_EXPERT_GROUPS)]

    sub = lax.broadcasted_iota(jnp.int32, (per_group, tm), 0).astype(F32)
    eidx = [sub + float(per_group * g) for g in range(N_EXPERT_GROUPS)]
    chosen = [jnp.zeros((per_group, tm), F32) for _ in range(N_EXPERT_GROUPS)]
    for _ in range(TOP_K):
        m = jnp.max(functools.reduce(jnp.maximum, vals), axis=0, keepdims=True)
        cand = [jnp.where(v == m, e, float(N_EXPERTS)) for v, e in zip(vals, eidx)]
        idx = jnp.min(functools.reduce(jnp.minimum, cand), axis=0, keepdims=True)
        for g in range(N_EXPERT_GROUPS):
            pick = eidx[g] == idx
            chosen[g] = jnp.where(pick, 1.0, chosen[g])
            vals[g] = jnp.where(pick, neg, vals[g])

    w = [jnp.where(c > 0.0, s, 0.0) for c, s in zip(chosen, sc_tiles)]
    denom = functools.reduce(jnp.add, [jnp.sum(t, axis=0, keepdims=True) for t in w])
    return jnp.concatenate([t / denom * ROUTED_SCALE for t in w], axis=0)


def _mix_kernel(x_ref, osb_ref, sga_ref, mb_ref, mod_ref, wsb_f32, wout_f32, g2_ref, wr_ref,
                rb_ref, wsg_f32, wsu_f32, wsd_f32, h2_ref, base_ref, gates_ref,
                wsb_ref, wout_ref, wsgu_ref, wsd_ref):
    @pl.when(pl.program_id(0) == 0)
    def _():
        wsb_ref[...] = wsb_f32[...].astype(BF16)
        wout_ref[...] = wout_f32[...].astype(BF16)
        wsgu_ref[:, :SHARED_DIM] = wsg_f32[...].astype(BF16)
        wsgu_ref[:, SHARED_DIM:] = wsu_f32[...].astype(BF16)
        wsd_ref[...] = wsd_f32[...].astype(BF16)

    m = mod_ref[0]
    gate1, shift2, scale2, gate2 = m[2:3], m[3:4], m[4:5], m[5:6]
    gain2 = g2_ref[...] * (1.0 + scale2)
    rows = x_ref.shape[0] // MIX_ROW_SPLITS
    for c in range(MIX_ROW_SPLITS):
        rs = slice(c * rows, (c + 1) * rows)
        ysb = _dot(osb_ref[rs, :], wsb_ref[...])
        merged = sga_ref[rs, :] * ysb.astype(BF16) + mb_ref[rs, :]
        x1 = x_ref[rs, :] + gate1 * _dot(merged, wout_ref[...])

        r = lax.rsqrt(jnp.mean(x1 * x1, axis=-1, keepdims=True) + NORM_EPS)
        h2 = (x1 * r) * gain2 + shift2
        h_hi, h_lo = _split_bf16(h2)
        h2_ref[rs, :] = h_hi

        su = _dot(h_hi, wsgu_ref[...])
        sh = _dot((_silu(su[:, :SHARED_DIM]) * su[:, SHARED_DIM:]).astype(BF16), wsd_ref[...])
        base_ref[rs, :] = x1 + gate2 * sh

        hi_both = _dot(h_hi, wr_ref[...])
        logits = (hi_both[:, :LANES] + hi_both[:, LANES:]) + _dot(h_lo, wr_ref[:, :LANES])
        scores = jax.nn.sigmoid(logits.T[:N_EXPERTS, :])
        gates = _route(scores + rb_ref[...], scores)
        gates_pad = jnp.concatenate([gates, jnp.zeros((LANES - N_EXPERTS, rows), F32)], axis=0)
        gates_ref[rs, :] = gates_pad.T


def _mix(x2, osb, sga, mb, mod3, wsb, wout, g2, wr, rb, wsg, wsu, wsd, seq):
    tm = TM_MIX
    tok = x2.shape[0]
    per_batch = seq // tm
    row = lambda i: (i, 0)
    const2 = lambda i: (0, 0)
    return pl.pallas_call(
        _mix_kernel,
        out_shape=(jax.ShapeDtypeStruct((tok, D_MODEL), BF16),
                   jax.ShapeDtypeStruct((tok, D_MODEL), F32),
                   jax.ShapeDtypeStruct((tok, LANES), F32)),
        grid=(tok // tm,),
        in_specs=[
            pl.BlockSpec((tm, D_MODEL), row),
            pl.BlockSpec((tm, SB_WIDTH), row),
            pl.BlockSpec((tm, D_MODEL), row),
            pl.BlockSpec((tm, D_MODEL), row),
            pl.BlockSpec((1, 6, D_MODEL), lambda i: (i // per_batch, 0, 0)),
            pl.BlockSpec(wsb.shape, const2),
            pl.BlockSpec(wout.shape, const2),
            pl.BlockSpec((1, D_MODEL), const2),
            pl.BlockSpec(wr.shape, const2),
            pl.BlockSpec(rb.shape, const2),
            pl.BlockSpec(wsg.shape, const2),
            pl.BlockSpec(wsu.shape, const2),
            pl.BlockSpec(wsd.shape, const2),
        ],
        out_specs=(pl.BlockSpec((tm, D_MODEL), row),
                   pl.BlockSpec((tm, D_MODEL), row),
                   pl.BlockSpec((tm, LANES), row)),
        scratch_shapes=[pltpu.VMEM(wsb.shape, BF16),
                        pltpu.VMEM(wout.shape, BF16),
                        pltpu.VMEM((D_MODEL, 2 * SHARED_DIM), BF16),
                        pltpu.VMEM(wsd.shape, BF16)],
        compiler_params=pltpu.CompilerParams(dimension_semantics=("arbitrary",),
                                             vmem_limit_bytes=VMEM_LIMIT),
        name="mix",
    )(x2, osb, sga, mb, mod3, wsb, wout, g2, wr, rb, wsg, wsu, wsd)


def _moe_kernel(h_ref, g_ref, base_ref, mod_ref, wg_ref, wu_ref, wd_ref, o_ref, hm_ref):
    n_groups = N_EXPERTS // EXPERTS_PER_STEP
    s = pl.program_id(0)
    last = pl.num_programs(0) - 1
    group_up = jnp.minimum(s, last - 1) % n_groups
    group_dn = jnp.maximum(s - 1, 0) % n_groups
    width = EXPERTS_PER_STEP * EXPERT_DIM

    @pl.when(s == 0)
    def _():
        hm_ref[1] = jnp.zeros(hm_ref.shape[1:], BF16)

    @pl.when(group_dn == 0)
    def _():
        o_ref[...] = jnp.zeros_like(o_ref)

    wd = wd_ref[...].astype(BF16).reshape(width, D_MODEL)
    wgu = jnp.concatenate([wg_ref[j].astype(BF16) for j in range(EXPERTS_PER_STEP)]
                          + [wu_ref[j].astype(BF16) for j in range(EXPERTS_PER_STEP)], axis=1)
    rows = h_ref.shape[0] // MOE_ROW_SPLITS
    lane = lax.broadcasted_iota(jnp.int32, (rows, LANES), 1)
    rd, wr = (s + 1) % 2, s % 2
    for c in range(MOE_ROW_SPLITS):
        rs = slice(c * rows, (c + 1) * rows)
        o_ref[rs, :] += _dot(hm_ref[rd, rs, :], wd)
    for c in range(MOE_ROW_SPLITS):
        rs = slice(c * rows, (c + 1) * rows)
        up = _dot(h_ref[rs, :], wgu)
        gates = g_ref[rs, :]
        parts = []
        for j in range(EXPERTS_PER_STEP):
            a = up[:, j * EXPERT_DIM:(j + 1) * EXPERT_DIM]
            b = up[:, width + j * EXPERT_DIM:width + (j + 1) * EXPERT_DIM]
            gate = jnp.sum(jnp.where(lane == group_up * EXPERTS_PER_STEP + j, gates, 0.0),
                           axis=1, keepdims=True)
            parts.append((_silu(a) * b * gate).astype(BF16))
        hm_ref[wr, rs, :] = jnp.concatenate(parts, axis=1)

    @pl.when(jnp.logical_and(s > 0, group_dn == n_groups - 1))
    def _():
        gate2 = mod_ref[0][5:6]
        o_ref[...] = base_ref[...] + gate2 * o_ref[...]


def _moe(h2, gates, base, mod3, wg, wu, wd, seq):
    tm = TM_MOE
    tok = h2.shape[0]
    per_batch = seq // tm
    g = EXPERTS_PER_STEP
    n_groups = N_EXPERTS // g
    n_steps = (tok // tm) * n_groups + 1
    up_step = lambda s: jnp.minimum(s, n_steps - 2)
    dn_step = lambda s: jnp.maximum(s - 1, 0)
    up_tile = lambda s: (up_step(s) // n_groups, 0)
    dn_tile = lambda s: (dn_step(s) // n_groups, 0)
    return pl.pallas_call(
        _moe_kernel,
        out_shape=jax.ShapeDtypeStruct((tok, D_MODEL), F32),
        grid=(n_steps,),
        in_specs=[
            pl.BlockSpec((tm, D_MODEL), up_tile, pipeline_mode=pl.Buffered(1)),
            pl.BlockSpec((tm, LANES), up_tile, pipeline_mode=pl.Buffered(1)),
            pl.BlockSpec((tm, D_MODEL), dn_tile, pipeline_mode=pl.Buffered(1)),
            pl.BlockSpec((1, 6, D_MODEL), lambda s: (dn_step(s) // n_groups // per_batch, 0, 0)),
            pl.BlockSpec((g, D_MODEL, EXPERT_DIM), lambda s: (up_step(s) % n_groups, 0, 0)),
            pl.BlockSpec((g, D_MODEL, EXPERT_DIM), lambda s: (up_step(s) % n_groups, 0, 0)),
            pl.BlockSpec((g, EXPERT_DIM, D_MODEL), lambda s: (dn_step(s) % n_groups, 0, 0)),
        ],
        out_specs=pl.BlockSpec((tm, D_MODEL), dn_tile),
        scratch_shapes=[pltpu.VMEM((2, tm, g * EXPERT_DIM), BF16)],
        compiler_params=pltpu.CompilerParams(dimension_semantics=("arbitrary",),
                                             vmem_limit_bytes=VMEM_LIMIT),
        name="moe",
    )(h2, gates, base, mod3, wg, wu, wd)


def _block_diag(blocks):
    n = len(blocks)
    rows = [jnp.concatenate([blocks[i] if i == j else jnp.zeros_like(blocks[j]) for j in range(n)], axis=1)
            for i in range(n)]
    return jnp.concatenate(rows, axis=0)


def _layer(x2, c_pad, batch, seq, w_ada, b_ada, g_norm1, w_in, g_q, g_k, w_sb_out, w_pool_group,
           pool_scale, w_pool_out, w_out, g_norm2, w_router, router_bias,
           w_exp_gate, w_exp_up, w_exp_down, w_sh_gate, w_sh_up, w_sh_down):
    mod = _ada(c_pad, w_ada, b_ada[None, :])
    mod3 = mod[:batch].reshape(batch, 6, D_MODEL)

    heads_per_tile = MXU_TILE // HEAD_DIM
    head_avg = np.kron(np.eye(heads_per_tile, dtype=np.float32),
                       np.full((HEAD_DIM, HEAD_DIM), 1.0 / HEAD_DIM, np.float32))
    head_avg = jnp.asarray(head_avg, dtype=BF16)
    gq = (jnp.tile(g_q, SB_HEADS) * (HEAD_DIM ** -0.5 * LOG2_E))[None, :]
    gk = jnp.tile(g_k, SB_HEADS)[None, :]
    wpg = w_pool_group.astype(BF16)
    wpg_pairs = jnp.stack([_block_diag([wpg[2 * i], wpg[2 * i + 1]])
                           for i in range(len(POOL_WINDOWS) // 2)])
    q, k, v, sga, mb = _inproj(
        x2, mod3, g_norm1[None, :], w_in, gq, gk, head_avg,
        wpg_pairs, pool_scale[None, :], w_pool_out, batch, seq)

    shape3 = (batch, seq, SB_WIDTH)
    osb = _attention(q.reshape(shape3), k.reshape(shape3), v.reshape(shape3)).reshape(batch * seq, SB_WIDTH)

    wr = jnp.pad(w_router, ((0, 0), (0, LANES - N_EXPERTS)))
    wr_hi = wr.astype(BF16)
    wr_hi_lo = jnp.concatenate([wr_hi, (wr - wr_hi.astype(F32)).astype(BF16)], axis=1)
    h2, base, gates = _mix(
        x2, osb, sga, mb, mod3, w_sb_out, w_out, g_norm2[None, :],
        wr_hi_lo, router_bias[:, None], w_sh_gate, w_sh_up, w_sh_down, seq)

    return _moe(h2, gates, base, mod3, w_exp_gate, w_exp_up, w_exp_down, seq)


def kernel(x, c, w_ada, b_ada, g_norm1, w_in, g_q, g_k, w_sb_out, w_pool_group, pool_scale, w_pool_out,
           w_out, g_norm2, w_router, router_bias, w_exp_gate, w_exp_up, w_exp_down,
           w_sh_gate, w_sh_up, w_sh_down):
    batch, seq, d = x.shape
    x2 = x.reshape(batch * seq, d)
    c_pad = jnp.pad(c, ((0, 8 - batch), (0, 0)))
    for l in range(w_ada.shape[0]):
        x2 = _layer(x2, c_pad, batch, seq, w_ada[l], b_ada[l], g_norm1[l], w_in[l], g_q[l], g_k[l],
                    w_sb_out[l], w_pool_group[l], pool_scale[l], w_pool_out[l], w_out[l], g_norm2[l],
                    w_router[l], router_bias[l], w_exp_gate[l], w_exp_up[l], w_exp_down[l],
                    w_sh_gate[l], w_sh_up[l], w_sh_down[l])
    return x2.reshape(batch, seq, d)
```

```python
import functools

import jax
import jax.numpy as jnp
import numpy as np
from jax import lax
from jax.experimental import pallas as pl
from jax.experimental.pallas import tpu as pltpu

D_MODEL = 1024
SB_HEADS = 8
HEAD_DIM = 64
SB_WIDTH = SB_HEADS * HEAD_DIM
POOL_WINDOWS = (2, 4, 8, 16)
POOL_WIDTH = 512
POOL_GROUP_DIM = POOL_WIDTH // len(POOL_WINDOWS)
MAX_WIN = max(POOL_WINDOWS)
N_EXPERTS = 64
EXPERT_DIM = 128
TOP_K = 8
N_EXPERT_GROUPS = 8
TOPK_GROUPS = 4
SHARED_DIM = 256
ROUTED_SCALE = 2.5
NORM_EPS = 1e-6

LANES = 128
MXU_TILE = 256
VMEM_LIMIT = 56 * 1024 * 1024

TM_IN = 1024
TM_MIX = 1024
MIX_ROW_SPLITS = 2
IN_ROW_SPLITS = 4
TM_MOE = 2048
MOE_ROW_SPLITS = 4
EXPERTS_PER_STEP = 4
ATTN_Q = 128
ATTN_UNROLL = 2
ATTN_K = 384
ATTN_SKIP_LOG2 = 128.0
MASKED_LOGIT = -1e30
LOG2_E = 1.4426950408889634

F32 = jnp.float32
BF16 = jnp.bfloat16


def _dot(a, b):
    return jnp.dot(a, b, preferred_element_type=F32)


def _split_bf16(x):
    hi = x.astype(BF16)
    lo = (x - hi.astype(F32)).astype(BF16)
    return hi, lo


def _silu(x):
    return x * jax.nn.sigmoid(x)


def _ada_kernel(c_ref, w_ref, b_ref, o_ref):
    c = c_ref[...]
    o_ref[...] = _dot(_silu(c).astype(BF16), w_ref[...].astype(BF16)) + b_ref[...]


def _ada(c_pad, w_ada, b_ada):
    n = w_ada.shape[1]
    tn = D_MODEL
    return pl.pallas_call(
        _ada_kernel,
        out_shape=jax.ShapeDtypeStruct((c_pad.shape[0], n), F32),
        grid=(n // tn,),
        in_specs=[
            pl.BlockSpec(c_pad.shape, lambda i: (0, 0)),
            pl.BlockSpec((D_MODEL, tn), lambda i: (0, i)),
            pl.BlockSpec((1, tn), lambda i: (0, i)),
        ],
        out_specs=pl.BlockSpec((c_pad.shape[0], tn), lambda i: (0, i)),
        compiler_params=pltpu.CompilerParams(dimension_semantics=("arbitrary",),
                                             vmem_limit_bytes=VMEM_LIMIT),
        name="ada",
    )(c_pad, w_ada, b_ada)


def _in_kernel(x_ref, mod_ref, g1_ref, w_ref, gq_ref, gk_ref, bd_ref, wpg_ref, ps_ref, wpo_ref,
               q_ref, k_ref, v_ref, sga_ref, mb_ref, ubuf, w_bf, wpo_bf):
    j = pl.program_id(1)
    tm = x_ref.shape[0]

    @pl.when(jnp.logical_and(pl.program_id(0) == 0, j == 0))
    def _():
        w_bf[...] = w_ref[...].astype(BF16)
        for gi in range(len(POOL_WINDOWS)):
            lo = gi * POOL_GROUP_DIM
            a_hi, a_lo = _split_bf16(wpg_ref[gi] * ps_ref[:, lo:lo + POOL_GROUP_DIM])
            b_hi, b_lo = _split_bf16(wpo_ref[lo:lo + POOL_GROUP_DIM, :])
            wpo_bf[lo:lo + POOL_GROUP_DIM, :] = (_dot(a_hi, b_hi) + _dot(a_hi, b_lo) + _dot(a_lo, b_hi)).astype(BF16)
    m = mod_ref[0]
    shift1, scale1 = m[0:1], m[1:2]
    gain1 = g1_ref[...] * (1.0 + scale1)

    def headnorm(t, g_ref):
        sq = (t * t).astype(BF16)
        half = bd_ref.shape[1]
        ms = jnp.concatenate([_dot(sq[:, c:c + half], bd_ref[...]) for c in range(0, SB_WIDTH, half)], axis=1)
        return (t * lax.rsqrt(ms + NORM_EPS)) * g_ref[...]

    @pl.when(j == 0)
    def _():
        ubuf[0:MAX_WIN, :] = jnp.zeros((MAX_WIN, POOL_WIDTH), F32)

    @pl.when(j > 0)
    def _():
        ubuf[0:MAX_WIN, :] = ubuf[tm:tm + MAX_WIN, :]

    rows = tm // IN_ROW_SPLITS
    u0 = 3 * SB_WIDTH
    g0 = u0 + POOL_WIDTH
    for c in range(IN_ROW_SPLITS):
        r0 = c * rows
        rs = slice(r0, r0 + rows)
        x = x_ref[rs, :]
        r = lax.rsqrt(jnp.mean(x * x, axis=-1, keepdims=True) + NORM_EPS)
        hb = ((x * r) * gain1 + shift1).astype(BF16)

        def proj(lo, hi):
            return _dot(hb, w_bf[:, lo:hi])

        q_ref[rs, :] = headnorm(proj(0, SB_WIDTH), gq_ref).astype(BF16)
        k_ref[rs, :] = headnorm(proj(SB_WIDTH, 2 * SB_WIDTH), gk_ref).astype(BF16)
        v_ref[rs, :] = proj(2 * SB_WIDTH, 3 * SB_WIDTH).astype(BF16)

        u = proj(u0, u0 + POOL_WIDTH)
        ubuf[MAX_WIN + r0:MAX_WIN + r0 + rows, :] = u
        t = j * tm + r0 + lax.broadcasted_iota(jnp.int32, (rows, POOL_GROUP_DIM), 0)
        mixed = []
        for gi, w in enumerate(POOL_WINDOWS):
            lo = gi * POOL_GROUP_DIM
            ug = u[:, lo:lo + POOL_GROUP_DIM]
            s = ug
            for i in range(1, w):
                s = s + ubuf[MAX_WIN + r0 - i:MAX_WIN + r0 - i + rows, lo:lo + POOL_GROUP_DIM]
            cnt = jnp.minimum(t + 1, w).astype(F32)
            mixed.append((s / cnt - ug).astype(BF16))
        ypool = _dot(jnp.concatenate(mixed, axis=1), wpo_bf[...])

        sga_ref[rs, :] = jax.nn.sigmoid(proj(g0, g0 + D_MODEL)).astype(BF16)
        mb_ref[rs, :] = (jax.nn.sigmoid(proj(g0 + D_MODEL, g0 + 2 * D_MODEL)) * ypool).astype(BF16)


def _inproj(x2, mod3, g1, w_in, gq, gk, bd, wpg, ps, wpo, batch, seq):
    tm = TM_IN
    nj = seq // tm
    tok = x2.shape[0]
    row = lambda b, j: (b * nj + j, 0)
    const2 = lambda b, j: (0, 0)
    out_sd = lambda n: jax.ShapeDtypeStruct((tok, n), BF16)
    return pl.pallas_call(
        _in_kernel,
        out_shape=(out_sd(SB_WIDTH), out_sd(SB_WIDTH), out_sd(SB_WIDTH), out_sd(D_MODEL), out_sd(D_MODEL)),
        grid=(batch, nj),
        in_specs=[
            pl.BlockSpec((tm, D_MODEL), row),
            pl.BlockSpec((1, 6, D_MODEL), lambda b, j: (b, 0, 0)),
            pl.BlockSpec((1, D_MODEL), const2),
            pl.BlockSpec(w_in.shape, const2),
            pl.BlockSpec((1, SB_WIDTH), const2),
            pl.BlockSpec((1, SB_WIDTH), const2),
            pl.BlockSpec(bd.shape, const2),
            pl.BlockSpec(wpg.shape, lambda b, j: (0, 0, 0)),
            pl.BlockSpec((1, POOL_WIDTH), const2),
            pl.BlockSpec((POOL_WIDTH, D_MODEL), const2),
        ],
        out_specs=(
            pl.BlockSpec((tm, SB_WIDTH), row),
            pl.BlockSpec((tm, SB_WIDTH), row),
            pl.BlockSpec((tm, SB_WIDTH), row),
            pl.BlockSpec((tm, D_MODEL), row),
            pl.BlockSpec((tm, D_MODEL), row),
        ),
        scratch_shapes=[pltpu.VMEM((MAX_WIN + tm, POOL_WIDTH), F32),
                        pltpu.VMEM(w_in.shape, BF16),
                        pltpu.VMEM(wpo.shape, BF16)],
        compiler_params=pltpu.CompilerParams(dimension_semantics=("arbitrary", "arbitrary"),
                                             vmem_limit_bytes=VMEM_LIMIT),
        name="inproj",
    )(x2, mod3, g1, w_in, gq, gk, bd, wpg, ps, wpo)


def _attn_kernel(q_ref, k_ref, v_ref, o_ref, tri_ref):
    tq, tk = ATTN_Q, ATTN_K
    seq = q_ref.shape[1]
    n_pairs = q_ref.shape[2] // LANES
    row = lax.broadcasted_iota(jnp.int32, (tk, tk), 0)
    col = lax.broadcasted_iota(jnp.int32, (tk, tk), 1)
    tri_ref[...] = jnp.where(row >= col, 1.0, 0.0).astype(BF16)
    kcol = lax.broadcasted_iota(jnp.int32, (tq, tk), 1)
    col_minus_row = kcol - lax.broadcasted_iota(jnp.int32, (tq, tk), 0)
    first_head = lax.broadcasted_iota(jnp.int32, (tq, LANES), 1) < HEAD_DIM
    first_head_k = lax.broadcasted_iota(jnp.int32, (tk, LANES), 1) < HEAD_DIM

    n_heads = 2 * n_pairs

    def window(q_pairs, start, mask, carry=None, accs=None):
        zs = []
        for p in range(n_pairs):
            kb = k_ref[0, pl.ds(start, tk), p * LANES:(p + 1) * LANES]
            zs.append(lax.dot_general(q_pairs[p], kb, (((1,), (1,)), ((), ())), preferred_element_type=F32))
        z = jnp.concatenate(zs, axis=0).reshape(n_heads, tq, tk)
        z = jnp.where(mask[None], z, MASKED_LOGIT)
        fail = jnp.maximum(z, 0.0) + jnp.log2(1.0 + jnp.exp2(-jnp.abs(z)))
        spent = _dot(fail.astype(BF16).reshape(n_heads * tq, tk), tri_ref[...]).reshape(n_heads, tq, tk)
        log_w = z - spent
        total = spent[:, :, 0:1]
        if carry is not None:
            log_w = log_w - carry
            total = total + carry
        a = jnp.exp2(log_w).astype(BF16)
        new_accs = []
        for p in range(n_pairs):
            vb = v_ref[0, pl.ds(start, tk), p * LANES:(p + 1) * LANES]
            zero = jnp.zeros_like(vb)
            v_diag = jnp.concatenate([jnp.where(first_head_k, vb, zero), jnp.where(first_head_k, zero, vb)],
                                     axis=0)
            av = _dot(jnp.concatenate([a[2 * p], a[2 * p + 1]], axis=-1), v_diag)
            new_accs.append(av if accs is None else accs[p] + av)
        return total, tuple(new_accs)

    def q_body(it, _):
        blocks = []
        for u in range(ATTN_UNROLL):
            qs = pl.multiple_of((it * ATTN_UNROLL + u) * tq, tq)
            ws = pl.multiple_of(jnp.maximum(qs - (tk - tq), 0), tq)
            causal = col_minus_row < (qs - ws)
            q_pairs = []
            for p in range(n_pairs):
                q2 = q_ref[0, pl.ds(qs, tq), p * LANES:(p + 1) * LANES]
                zero = jnp.zeros_like(q2)
                q_pairs.append(jnp.concatenate([jnp.where(first_head, q2, zero),
                                                jnp.where(first_head, zero, q2)], axis=0))
            carry, accs = window(q_pairs, ws, causal)
            blocks.append((qs, ws, q_pairs, carry, accs))

        for qs, ws, q_pairs, carry, accs in blocks:
            def cond(st):
                start, carry, _ = st
                return jnp.logical_and(start > 0, jnp.min(carry) < ATTN_SKIP_LOG2)

            def body(st, q_pairs=q_pairs):
                start, carry, accs = st
                nxt = pl.multiple_of(jnp.maximum(start - tk, 0), tq)
                carry, accs = window(q_pairs, nxt, kcol < (start - nxt), carry, accs)
                return nxt, carry, accs

            _, _, accs = lax.while_loop(cond, body, (ws, carry, accs))
            o_ref[0, pl.ds(qs, tq), :] = jnp.concatenate(accs, axis=1).astype(BF16)
        return 0

    lax.fori_loop(0, seq // (tq * ATTN_UNROLL), q_body, 0)


def _attention(q3, k3, v3):
    batch, seq, width = q3.shape
    spec = pl.BlockSpec((1, seq, width), lambda b: (b, 0, 0))
    return pl.pallas_call(
        _attn_kernel,
        out_shape=jax.ShapeDtypeStruct((batch, seq, width), BF16),
        grid=(batch,),
        in_specs=[spec, spec, spec],
        out_specs=spec,
        scratch_shapes=[pltpu.VMEM((ATTN_K, ATTN_K), BF16)],
        compiler_params=pltpu.CompilerParams(dimension_semantics=("arbitrary",),
                                             vmem_limit_bytes=VMEM_LIMIT),
        name="attn",
    )(q3, k3, v3)


def _route(sel, scores):
    per_group = N_EXPERTS // N_EXPERT_GROUPS
    tm = sel.shape[1]
    neg = -jnp.inf
    tiles = [sel[per_group * g:per_group * (g + 1), :] for g in range(N_EXPERT_GROUPS)]
    sc_tiles = [scores[per_group * g:per_group * (g + 1), :] for g in range(N_EXPERT_GROUPS)]

    grp = []
    for tg in tiles:
        m1 = jnp.max(tg, axis=0, keepdims=True)
        eq = tg == m1
        n_eq = jnp.sum(jnp.where(eq, 1.0, 0.0), axis=0, keepdims=True)
        second = jnp.max(jnp.where(eq, neg, tg), axis=0, keepdims=True)
        grp.append(m1 + jnp.where(n_eq >= 2.0, m1, second))

    gtile = jnp.concatenate(grp, axis=0)
    gidx = lax.broadcasted_iota(jnp.int32, (N_EXPERT_GROUPS, tm), 0)
    rank = jnp.zeros((N_EXPERT_GROUPS, tm), F32)
    for gp in range(N_EXPERT_GROUPS):
        other = jnp.broadcast_to(grp[gp], (N_EXPERT_GROUPS, tm))
        rank = rank + jnp.where(other > gtile, 1.0, 0.0)
        rank = rank + jnp.where(jnp.logical_and(other == gtile, gidx > gp), 1.0, 0.0)
    keep = jnp.where(rank < float(TOPK_GROUPS), 1.0, 0.0)
    vals = [jnp.where(jnp.broadcast_to(keep[g:g + 1, :], (per_group, tm)) > 0.0, tiles[g], neg)
            for g in range(N_EXPERT_GROUPS)]

    sub = lax.broadcasted_iota(jnp.int32, (per_group, tm), 0).astype(F32)
    eidx = [sub + float(per_group * g) for g in range(N_EXPERT_GROUPS)]
    chosen = [jnp.zeros((per_group, tm), F32) for _ in range(N_EXPERT_GROUPS)]
    for _ in range(TOP_K):
        m = jnp.max(functools.reduce(jnp.maximum, vals), axis=0, keepdims=True)
        cand = [jnp.where(v == m, e, float(N_EXPERTS)) for v, e in zip(vals, eidx)]
        idx = jnp.min(functools.reduce(jnp.minimum, cand), axis=0, keepdims=True)
        for g in range(N_EXPERT_GROUPS):
            pick = eidx[g] == idx
            chosen[g] = jnp.where(pick, 1.0, chosen[g])
            vals[g] = jnp.where(pick, neg, vals[g])

    w = [jnp.where(c > 0.0, s, 0.0) for c, s in zip(chosen, sc_tiles)]
    denom = functools.reduce(jnp.add, [jnp.sum(t, axis=0, keepdims=True) for t in w])
    return jnp.concatenate([t / denom * ROUTED_SCALE for t in w], axis=0)


def _mix_kernel(x_ref, osb_ref, sga_ref, mb_ref, mod_ref, wsb_f32, wout_f32, g2_ref, wr_ref,
                rb_ref, wsg_f32, wsu_f32, wsd_f32, h2_ref, base_ref, gates_ref,
                wsb_ref, wout_ref, wsgu_ref, wsd_ref):
    @pl.when(pl.program_id(0) == 0)
    def _():
        wsb_ref[...] = wsb_f32[...].astype(BF16)
        wout_ref[...] = wout_f32[...].astype(BF16)
        wsgu_ref[:, :SHARED_DIM] = wsg_f32[...].astype(BF16)
        wsgu_ref[:, SHARED_DIM:] = wsu_f32[...].astype(BF16)
        wsd_ref[...] = wsd_f32[...].astype(BF16)

    m = mod_ref[0]
    gate1, shift2, scale2, gate2 = m[2:3], m[3:4], m[4:5], m[5:6]
    gain2 = g2_ref[...] * (1.0 + scale2)
    rows = x_ref.shape[0] // MIX_ROW_SPLITS
    for c in range(MIX_ROW_SPLITS):
        rs = slice(c * rows, (c + 1) * rows)
        ysb = _dot(osb_ref[rs, :], wsb_ref[...])
        merged = sga_ref[rs, :] * ysb.astype(BF16) + mb_ref[rs, :]
        x1 = x_ref[rs, :] + gate1 * _dot(merged, wout_ref[...])

        r = lax.rsqrt(jnp.mean(x1 * x1, axis=-1, keepdims=True) + NORM_EPS)
        h2 = (x1 * r) * gain2 + shift2
        h_hi, h_lo = _split_bf16(h2)
        h2_ref[rs, :] = h_hi

        su = _dot(h_hi, wsgu_ref[...])
        sh = _dot((_silu(su[:, :SHARED_DIM]) * su[:, SHARED_DIM:]).astype(BF16), wsd_ref[...])
        base_ref[rs, :] = x1 + gate2 * sh

        hi_both = _dot(h_hi, wr_ref[...])
        logits = (hi_both[:, :LANES] + hi_both[:, LANES:]) + _dot(h_lo, wr_ref[:, :LANES])
        scores = jax.nn.sigmoid(logits.T[:N_EXPERTS, :])
        gates = _route(scores + rb_ref[...], scores)
        gates_pad = jnp.concatenate([gates, jnp.zeros((LANES - N_EXPERTS, rows), F32)], axis=0)
        gates_ref[rs, :] = gates_pad.T


def _mix(x2, osb, sga, mb, mod3, wsb, wout, g2, wr, rb, wsg, wsu, wsd, seq):
    tm = TM_MIX
    tok = x2.shape[0]
    per_batch = seq // tm
    row = lambda i: (i, 0)
    const2 = lambda i: (0, 0)
    return pl.pallas_call(
        _mix_kernel,
        out_shape=(jax.ShapeDtypeStruct((tok, D_MODEL), BF16),
                   jax.ShapeDtypeStruct((tok, D_MODEL), F32),
                   jax.ShapeDtypeStruct((tok, LANES), F32)),
        grid=(tok // tm,),
        in_specs=[
            pl.BlockSpec((tm, D_MODEL), row),
            pl.BlockSpec((tm, SB_WIDTH), row),
            pl.BlockSpec((tm, D_MODEL), row),
            pl.BlockSpec((tm, D_MODEL), row),
            pl.BlockSpec((1, 6, D_MODEL), lambda i: (i // per_batch, 0, 0)),
            pl.BlockSpec(wsb.shape, const2),
            pl.BlockSpec(wout.shape, const2),
            pl.BlockSpec((1, D_MODEL), const2),
            pl.BlockSpec(wr.shape, const2),
            pl.BlockSpec(rb.shape, const2),
            pl.BlockSpec(wsg.shape, const2),
            pl.BlockSpec(wsu.shape, const2),
            pl.BlockSpec(wsd.shape, const2),
        ],
        out_specs=(pl.BlockSpec((tm, D_MODEL), row),
                   pl.BlockSpec((tm, D_MODEL), row),
                   pl.BlockSpec((tm, LANES), row)),
        scratch_shapes=[pltpu.VMEM(wsb.shape, BF16),
                        pltpu.VMEM(wout.shape, BF16),
                        pltpu.VMEM((D_MODEL, 2 * SHARED_DIM), BF16),
                        pltpu.VMEM(wsd.shape, BF16)],
        compiler_params=pltpu.CompilerParams(dimension_semantics=("arbitrary",),
                                             vmem_limit_bytes=VMEM_LIMIT),
        name="mix",
    )(x2, osb, sga, mb, mod3, wsb, wout, g2, wr, rb, wsg, wsu, wsd)


def _moe_kernel(h_ref, g_ref, base_ref, mod_ref, wg_ref, wu_ref, wd_ref, o_ref, hm_ref):
    n_groups = N_EXPERTS // EXPERTS_PER_STEP
    s = pl.program_id(0)
    last = pl.num_programs(0) - 1
    group_up = jnp.minimum(s, last - 1) % n_groups
    group_dn = jnp.maximum(s - 1, 0) % n_groups
    width = EXPERTS_PER_STEP * EXPERT_DIM

    @pl.when(s == 0)
    def _():
        hm_ref[1] = jnp.zeros(hm_ref.shape[1:], BF16)

    @pl.when(group_dn == 0)
    def _():
        o_ref[...] = jnp.zeros_like(o_ref)

    wd = wd_ref[...].astype(BF16).reshape(width, D_MODEL)
    wgu = jnp.concatenate([wg_ref[j].astype(BF16) for j in range(EXPERTS_PER_STEP)]
                          + [wu_ref[j].astype(BF16) for j in range(EXPERTS_PER_STEP)], axis=1)
    rows = h_ref.shape[0] // MOE_ROW_SPLITS
    lane = lax.broadcasted_iota(jnp.int32, (rows, LANES), 1)
    rd, wr = (s + 1) % 2, s % 2
    for c in range(MOE_ROW_SPLITS):
        rs = slice(c * rows, (c + 1) * rows)
        o_ref[rs, :] += _dot(hm_ref[rd, rs, :], wd)
    for c in range(MOE_ROW_SPLITS):
        rs = slice(c * rows, (c + 1) * rows)
        up = _dot(h_ref[rs, :], wgu)
        gates = g_ref[rs, :]
        parts = []
        for j in range(EXPERTS_PER_STEP):
            a = up[:, j * EXPERT_DIM:(j + 1) * EXPERT_DIM]
            b = up[:, width + j * EXPERT_DIM:width + (j + 1) * EXPERT_DIM]
            gate = jnp.sum(jnp.where(lane == group_up * EXPERTS_PER_STEP + j, gates, 0.0),
                           axis=1, keepdims=True)
            parts.append((_silu(a) * b * gate).astype(BF16))
        hm_ref[wr, rs, :] = jnp.concatenate(parts, axis=1)

    @pl.when(jnp.logical_and(s > 0, group_dn == n_groups - 1))
    def _():
        gate2 = mod_ref[0][5:6]
        o_ref[...] = base_ref[...] + gate2 * o_ref[...]


def _moe(h2, gates, base, mod3, wg, wu, wd, seq):
    tm = TM_MOE
    tok = h2.shape[0]
    per_batch = seq // tm
    g = EXPERTS_PER_STEP
    n_groups = N_EXPERTS // g
    n_steps = (tok // tm) * n_groups + 1
    up_step = lambda s: jnp.minimum(s, n_steps - 2)
    dn_step = lambda s: jnp.maximum(s - 1, 0)
    up_tile = lambda s: (up_step(s) // n_groups, 0)
    dn_tile = lambda s: (dn_step(s) // n_groups, 0)
    return pl.pallas_call(
        _moe_kernel,
        out_shape=jax.ShapeDtypeStruct((tok, D_MODEL), F32),
        grid=(n_steps,),
        in_specs=[
            pl.BlockSpec((tm, D_MODEL), up_tile, pipeline_mode=pl.Buffered(1)),
            pl.BlockSpec((tm, LANES), up_tile, pipeline_mode=pl.Buffered(1)),
            pl.BlockSpec((tm, D_MODEL), dn_tile, pipeline_mode=pl.Buffered(1)),
            pl.BlockSpec((1, 6, D_MODEL), lambda s: (dn_step(s) // n_groups // per_batch, 0, 0)),
            pl.BlockSpec((g, D_MODEL, EXPERT_DIM), lambda s: (up_step(s) % n_groups, 0, 0)),
            pl.BlockSpec((g, D_MODEL, EXPERT_DIM), lambda s: (up_step(s) % n_groups, 0, 0)),
            pl.BlockSpec((g, EXPERT_DIM, D_MODEL), lambda s: (dn_step(s) % n_groups, 0, 0)),
        ],
        out_specs=pl.BlockSpec((tm, D_MODEL), dn_tile),
        scratch_shapes=[pltpu.VMEM((2, tm, g * EXPERT_DIM), BF16)],
        compiler_params=pltpu.CompilerParams(dimension_semantics=("arbitrary",),
                                             vmem_limit_bytes=VMEM_LIMIT),
        name="moe",
    )(h2, gates, base, mod3, wg, wu, wd)


def _layer(x2, c_pad, batch, seq, w_ada, b_ada, g_norm1, w_in, g_q, g_k, w_sb_out, w_pool_group,
           pool_scale, w_pool_out, w_out, g_norm2, w_router, router_bias,
           w_exp_gate, w_exp_up, w_exp_down, w_sh_gate, w_sh_up, w_sh_down):
    mod = _ada(c_pad, w_ada, b_ada[None, :])
    mod3 = mod[:batch].reshape(batch, 6, D_MODEL)

    heads_per_tile = MXU_TILE // HEAD_DIM
    head_avg = np.kron(np.eye(heads_per_tile, dtype=np.float32),
                       np.full((HEAD_DIM, HEAD_DIM), 1.0 / HEAD_DIM, np.float32))
    head_avg = jnp.asarray(head_avg, dtype=BF16)
    gq = (jnp.tile(g_q, SB_HEADS) * (HEAD_DIM ** -0.5 * LOG2_E))[None, :]
    gk = jnp.tile(g_k, SB_HEADS)[None, :]
    q, k, v, sga, mb = _inproj(
        x2, mod3, g_norm1[None, :], w_in, gq, gk, head_avg,
        w_pool_group, pool_scale[None, :], w_pool_out, batch, seq)

    shape3 = (batch, seq, SB_WIDTH)
    osb = _attention(q.reshape(shape3), k.reshape(shape3), v.reshape(shape3)).reshape(batch * seq, SB_WIDTH)

    wr = jnp.pad(w_router, ((0, 0), (0, LANES - N_EXPERTS)))
    wr_hi = wr.astype(BF16)
    wr_hi_lo = jnp.concatenate([wr_hi, (wr - wr_hi.astype(F32)).astype(BF16)], axis=1)
    h2, base, gates = _mix(
        x2, osb, sga, mb, mod3, w_sb_out, w_out, g_norm2[None, :],
        wr_hi_lo, router_bias[:, None], w_sh_gate, w_sh_up, w_sh_down, seq)

    return _moe(h2, gates, base, mod3, w_exp_gate, w_exp_up, w_exp_down, seq)


def kernel(x, c, w_ada, b_ada, g_norm1, w_in, g_q, g_k, w_sb_out, w_pool_group, pool_scale, w_pool_out,
           w_out, g_norm2, w_router, router_bias, w_exp_gate, w_exp_up, w_exp_down,
           w_sh_gate, w_sh_up, w_sh_down):
    batch, seq, d = x.shape
    x2 = x.reshape(batch * seq, d)
    c_pad = jnp.pad(c, ((0, 8 - batch), (0, 0)))
    for l in range(w_ada.shape[0]):
        x2 = _layer(x2, c_pad, batch, seq, w_ada[l], b_ada[l], g_norm1[l], w_in[l], g_q[l], g_k[l],
                    w_sb_out[l], w_pool_group[l], pool_scale[l], w_pool_out[l], w_out[l], g_norm2[l],
                    w_router[l], router_bias[l], w_exp_gate[l], w_exp_up[l], w_exp_down[l],
                    w_sh_gate[l], w_sh_up[l], w_sh_down[l])
    return x2.reshape(batch, seq, d)
```

```python
import functools

import jax
import jax.numpy as jnp
import numpy as np
from jax import lax
from jax.experimental import pallas as pl
from jax.experimental.pallas import tpu as pltpu

D_MODEL = 1024
SB_HEADS = 8
HEAD_DIM = 64
SB_WIDTH = SB_HEADS * HEAD_DIM
POOL_WINDOWS = (2, 4, 8, 16)
POOL_WIDTH = 512
POOL_GROUP_DIM = POOL_WIDTH // len(POOL_WINDOWS)
MAX_WIN = max(POOL_WINDOWS)
N_EXPERTS = 64
EXPERT_DIM = 128
TOP_K = 8
N_EXPERT_GROUPS = 8
TOPK_GROUPS = 4
SHARED_DIM = 256
ROUTED_SCALE = 2.5
NORM_EPS = 1e-6

LANES = 128
MXU_TILE = 256
VMEM_LIMIT = 56 * 1024 * 1024

TM_IN = 1024
TM_MIX = 1024
MIX_ROW_SPLITS = 2
IN_ROW_SPLITS = 2
TM_MOE = 2048
MOE_ROW_SPLITS = 2
EXPERTS_PER_STEP = 4
ATTN_Q = 128
ATTN_UNROLL = 4
ATTN_K = 384
ATTN_SKIP_LOG2 = 128.0
MASKED_LOGIT = -1e30
LOG2_E = 1.4426950408889634

F32 = jnp.float32
BF16 = jnp.bfloat16


def _dot(a, b):
    return jnp.dot(a, b, preferred_element_type=F32)


def _split_bf16(x):
    hi = x.astype(BF16)
    lo = (x - hi.astype(F32)).astype(BF16)
    return hi, lo


def _silu(x):
    return x * jax.nn.sigmoid(x)


def _ada_kernel(c_ref, w_ref, b_ref, o_ref):
    c = c_ref[...]
    o_ref[...] = _dot(_silu(c).astype(BF16), w_ref[...].astype(BF16)) + b_ref[...]


def _ada(c_pad, w_ada, b_ada):
    n = w_ada.shape[1]
    tn = D_MODEL
    return pl.pallas_call(
        _ada_kernel,
        out_shape=jax.ShapeDtypeStruct((c_pad.shape[0], n), F32),
        grid=(n // tn,),
        in_specs=[
            pl.BlockSpec(c_pad.shape, lambda i: (0, 0)),
            pl.BlockSpec((D_MODEL, tn), lambda i: (0, i)),
            pl.BlockSpec((1, tn), lambda i: (0, i)),
        ],
        out_specs=pl.BlockSpec((c_pad.shape[0], tn), lambda i: (0, i)),
        compiler_params=pltpu.CompilerParams(dimension_semantics=("arbitrary",),
                                             vmem_limit_bytes=VMEM_LIMIT),
        name="ada",
    )(c_pad, w_ada, b_ada)


def _in_kernel(x_ref, mod_ref, g1_ref, w_ref, gq_ref, gk_ref, bd_ref, wpg_ref, ps_ref, wpo_ref,
               q_ref, k_ref, v_ref, sga_ref, mb_ref, ubuf, w_bf, wpo_bf):
    j = pl.program_id(1)
    tm = x_ref.shape[0]

    @pl.when(jnp.logical_and(pl.program_id(0) == 0, j == 0))
    def _():
        w_bf[...] = w_ref[...].astype(BF16)
        for gi in range(len(POOL_WINDOWS)):
            lo = gi * POOL_GROUP_DIM
            a_hi, a_lo = _split_bf16(wpg_ref[gi] * ps_ref[:, lo:lo + POOL_GROUP_DIM])
            b_hi, b_lo = _split_bf16(wpo_ref[lo:lo + POOL_GROUP_DIM, :])
            wpo_bf[lo:lo + POOL_GROUP_DIM, :] = (_dot(a_hi, b_hi) + _dot(a_hi, b_lo) + _dot(a_lo, b_hi)).astype(BF16)
    m = mod_ref[0]
    shift1, scale1 = m[0:1], m[1:2]
    gain1 = g1_ref[...] * (1.0 + scale1)

    def headnorm(t, g_ref):
        sq = (t * t).astype(BF16)
        half = bd_ref.shape[1]
        ms = jnp.concatenate([_dot(sq[:, c:c + half], bd_ref[...]) for c in range(0, SB_WIDTH, half)], axis=1)
        return (t * lax.rsqrt(ms + NORM_EPS)) * g_ref[...]

    @pl.when(j == 0)
    def _():
        ubuf[0:MAX_WIN, :] = jnp.zeros((MAX_WIN, POOL_WIDTH), F32)

    @pl.when(j > 0)
    def _():
        ubuf[0:MAX_WIN, :] = ubuf[tm:tm + MAX_WIN, :]

    rows = tm // IN_ROW_SPLITS
    u0 = 3 * SB_WIDTH
    g0 = u0 + POOL_WIDTH
    for c in range(IN_ROW_SPLITS):
        r0 = c * rows
        rs = slice(r0, r0 + rows)
        x = x_ref[rs, :]
        r = lax.rsqrt(jnp.mean(x * x, axis=-1, keepdims=True) + NORM_EPS)
        hb = ((x * r) * gain1 + shift1).astype(BF16)

        def proj(lo, hi):
            return _dot(hb, w_bf[:, lo:hi])

        q_ref[rs, :] = headnorm(proj(0, SB_WIDTH), gq_ref).astype(BF16)
        k_ref[rs, :] = headnorm(proj(SB_WIDTH, 2 * SB_WIDTH), gk_ref).astype(BF16)
        v_ref[rs, :] = proj(2 * SB_WIDTH, 3 * SB_WIDTH).astype(BF16)

        u = proj(u0, u0 + POOL_WIDTH)
        ubuf[MAX_WIN + r0:MAX_WIN + r0 + rows, :] = u
        t = j * tm + r0 + lax.broadcasted_iota(jnp.int32, (rows, POOL_GROUP_DIM), 0)
        mixed = []
        for gi, w in enumerate(POOL_WINDOWS):
            lo = gi * POOL_GROUP_DIM
            ug = u[:, lo:lo + POOL_GROUP_DIM]
            s = ug
            for i in range(1, w):
                s = s + ubuf[MAX_WIN + r0 - i:MAX_WIN + r0 - i + rows, lo:lo + POOL_GROUP_DIM]
            cnt = jnp.minimum(t + 1, w).astype(F32)
            mixed.append((s / cnt - ug).astype(BF16))
        ypool = _dot(jnp.concatenate(mixed, axis=1), wpo_bf[...])

        sga_ref[rs, :] = jax.nn.sigmoid(proj(g0, g0 + D_MODEL)).astype(BF16)
        mb_ref[rs, :] = (jax.nn.sigmoid(proj(g0 + D_MODEL, g0 + 2 * D_MODEL)) * ypool).astype(BF16)


def _inproj(x2, mod3, g1, w_in, gq, gk, bd, wpg, ps, wpo, batch, seq):
    tm = TM_IN
    nj = seq // tm
    tok = x2.shape[0]
    row = lambda b, j: (b * nj + j, 0)
    const2 = lambda b, j: (0, 0)
    out_sd = lambda n: jax.ShapeDtypeStruct((tok, n), BF16)
    return pl.pallas_call(
        _in_kernel,
        out_shape=(out_sd(SB_WIDTH), out_sd(SB_WIDTH), out_sd(SB_WIDTH), out_sd(D_MODEL), out_sd(D_MODEL)),
        grid=(batch, nj),
        in_specs=[
            pl.BlockSpec((tm, D_MODEL), row),
            pl.BlockSpec((1, 6, D_MODEL), lambda b, j: (b, 0, 0)),
            pl.BlockSpec((1, D_MODEL), const2),
            pl.BlockSpec(w_in.shape, const2),
            pl.BlockSpec((1, SB_WIDTH), const2),
            pl.BlockSpec((1, SB_WIDTH), const2),
            pl.BlockSpec(bd.shape, const2),
            pl.BlockSpec(wpg.shape, lambda b, j: (0, 0, 0)),
            pl.BlockSpec((1, POOL_WIDTH), const2),
            pl.BlockSpec((POOL_WIDTH, D_MODEL), const2),
        ],
        out_specs=(
            pl.BlockSpec((tm, SB_WIDTH), row),
            pl.BlockSpec((tm, SB_WIDTH), row),
            pl.BlockSpec((tm, SB_WIDTH), row),
            pl.BlockSpec((tm, D_MODEL), row),
            pl.BlockSpec((tm, D_MODEL), row),
        ),
        scratch_shapes=[pltpu.VMEM((MAX_WIN + tm, POOL_WIDTH), F32),
                        pltpu.VMEM(w_in.shape, BF16),
                        pltpu.VMEM(wpo.shape, BF16)],
        compiler_params=pltpu.CompilerParams(dimension_semantics=("arbitrary", "arbitrary"),
                                             vmem_limit_bytes=VMEM_LIMIT),
        name="inproj",
    )(x2, mod3, g1, w_in, gq, gk, bd, wpg, ps, wpo)


def _attn_kernel(q_ref, k_ref, v_ref, o_ref, tri_ref):
    tq, tk = ATTN_Q, ATTN_K
    seq = q_ref.shape[1]
    n_pairs = q_ref.shape[2] // LANES
    row = lax.broadcasted_iota(jnp.int32, (tk, tk), 0)
    col = lax.broadcasted_iota(jnp.int32, (tk, tk), 1)
    tri_ref[...] = jnp.where(row >= col, 1.0, 0.0).astype(BF16)
    kcol = lax.broadcasted_iota(jnp.int32, (tq, tk), 1)
    col_minus_row = kcol - lax.broadcasted_iota(jnp.int32, (tq, tk), 0)
    first_head = lax.broadcasted_iota(jnp.int32, (tq, LANES), 1) < HEAD_DIM
    first_head_k = lax.broadcasted_iota(jnp.int32, (tk, LANES), 1) < HEAD_DIM

    n_heads = 2 * n_pairs

    def window(q_pairs, start, mask, carry=None, accs=None):
        zs = []
        for p in range(n_pairs):
            kb = k_ref[0, pl.ds(start, tk), p * LANES:(p + 1) * LANES]
            zs.append(lax.dot_general(q_pairs[p], kb, (((1,), (1,)), ((), ())), preferred_element_type=F32))
        z = jnp.concatenate(zs, axis=0).reshape(n_heads, tq, tk)
        z = jnp.where(mask[None], z, MASKED_LOGIT)
        fail = jnp.maximum(z, 0.0) + jnp.log2(1.0 + jnp.exp2(-jnp.abs(z)))
        spent = _dot(fail.astype(BF16).reshape(n_heads * tq, tk), tri_ref[...]).reshape(n_heads, tq, tk)
        log_w = z - spent
        total = spent[:, :, 0:1]
        if carry is not None:
            log_w = log_w - carry
            total = total + carry
        a = jnp.exp2(log_w).astype(BF16)
        new_accs = []
        for p in range(n_pairs):
            vb = v_ref[0, pl.ds(start, tk), p * LANES:(p + 1) * LANES]
            zero = jnp.zeros_like(vb)
            v_diag = jnp.concatenate([jnp.where(first_head_k, vb, zero), jnp.where(first_head_k, zero, vb)],
                                     axis=0)
            av = _dot(jnp.concatenate([a[2 * p], a[2 * p + 1]], axis=-1), v_diag)
            new_accs.append(av if accs is None else accs[p] + av)
        return total, tuple(new_accs)

    def q_body(it, _):
        blocks = []
        for u in range(ATTN_UNROLL):
            qs = pl.multiple_of((it * ATTN_UNROLL + u) * tq, tq)
            ws = pl.multiple_of(jnp.maximum(qs - (tk - tq), 0), tq)
            causal = col_minus_row < (qs - ws)
            q_pairs = []
            for p in range(n_pairs):
                q2 = q_ref[0, pl.ds(qs, tq), p * LANES:(p + 1) * LANES]
                zero = jnp.zeros_like(q2)
                q_pairs.append(jnp.concatenate([jnp.where(first_head, q2, zero),
                                                jnp.where(first_head, zero, q2)], axis=0))
            carry, accs = window(q_pairs, ws, causal)
            blocks.append((qs, ws, q_pairs, carry, accs))

        for qs, ws, q_pairs, carry, accs in blocks:
            def cond(st):
                start, carry, _ = st
                return jnp.logical_and(start > 0, jnp.min(carry) < ATTN_SKIP_LOG2)

            def body(st, q_pairs=q_pairs):
                start, carry, accs = st
                nxt = pl.multiple_of(jnp.maximum(start - tk, 0), tq)
                carry, accs = window(q_pairs, nxt, kcol < (start - nxt), carry, accs)
                return nxt, carry, accs

            _, _, accs = lax.while_loop(cond, body, (ws, carry, accs))
            o_ref[0, pl.ds(qs, tq), :] = jnp.concatenate(accs, axis=1).astype(BF16)
        return 0

    lax.fori_loop(0, seq // (tq * ATTN_UNROLL), q_body, 0)


def _attention(q3, k3, v3):
    batch, seq, width = q3.shape
    spec = pl.BlockSpec((1, seq, width), lambda b: (b, 0, 0))
    return pl.pallas_call(
        _attn_kernel,
        out_shape=jax.ShapeDtypeStruct((batch, seq, width), BF16),
        grid=(batch,),
        in_specs=[spec, spec, spec],
        out_specs=spec,
        scratch_shapes=[pltpu.VMEM((ATTN_K, ATTN_K), BF16)],
        compiler_params=pltpu.CompilerParams(dimension_semantics=("arbitrary",),
                                             vmem_limit_bytes=VMEM_LIMIT),
        name="attn",
    )(q3, k3, v3)


def _route(sel, scores):
    per_group = N_EXPERTS // N_EXPERT_GROUPS
    tm = sel.shape[1]
    neg = -jnp.inf
    tiles = [sel[per_group * g:per_group * (g + 1), :] for g in range(N_EXPERT_GROUPS)]
    sc_tiles = [scores[per_group * g:per_group * (g + 1), :] for g in range(N_EXPERT_GROUPS)]

    grp = []
    for tg in tiles:
        m1 = jnp.max(tg, axis=0, keepdims=True)
        eq = tg == m1
        n_eq = jnp.sum(jnp.where(eq, 1.0, 0.0), axis=0, keepdims=True)
        second = jnp.max(jnp.where(eq, neg, tg), axis=0, keepdims=True)
        grp.append(m1 + jnp.where(n_eq >= 2.0, m1, second))

    gtile = jnp.concatenate(grp, axis=0)
    gidx = lax.broadcasted_iota(jnp.int32, (N_EXPERT_GROUPS, tm), 0)
    rank = jnp.zeros((N_EXPERT_GROUPS, tm), F32)
    for gp in range(N_EXPERT_GROUPS):
        other = jnp.broadcast_to(grp[gp], (N_EXPERT_GROUPS, tm))
        rank = rank + jnp.where(other > gtile, 1.0, 0.0)
        rank = rank + jnp.where(jnp.logical_and(other == gtile, gidx > gp), 1.0, 0.0)
    keep = jnp.where(rank < float(TOPK_GROUPS), 1.0, 0.0)
    vals = [jnp.where(jnp.broadcast_to(keep[g:g + 1, :], (per_group, tm)) > 0.0, tiles[g], neg)
            for g in range(N_EXPERT_GROUPS)]

    sub = lax.broadcasted_iota(jnp.int32, (per_group, tm), 0).astype(F32)
    eidx = [sub + float(per_group * g) for g in range(N_EXPERT_GROUPS)]
    chosen = [jnp.zeros((per_group, tm), F32) for _ in range(N_EXPERT_GROUPS)]
    for _ in range(TOP_K):
        m = jnp.max(functools.reduce(jnp.maximum, vals), axis=0, keepdims=True)
        cand = [jnp.where(v == m, e, float(N_EXPERTS)) for v, e in zip(vals, eidx)]
        idx = jnp.min(functools.reduce(jnp.minimum, cand), axis=0, keepdims=True)
        for g in range(N_EXPERT_GROUPS):
            pick = eidx[g] == idx
            chosen[g] = jnp.where(pick, 1.0, chosen[g])
            vals[g] = jnp.where(pick, neg, vals[g])

    w = [jnp.where(c > 0.0, s, 0.0) for c, s in zip(chosen, sc_tiles)]
    denom = functools.reduce(jnp.add, [jnp.sum(t, axis=0, keepdims=True) for t in w])
    return jnp.concatenate([t / denom * ROUTED_SCALE for t in w], axis=0)


def _mix_kernel(x_ref, osb_ref, sga_ref, mb_ref, mod_ref, wsb_f32, wout_f32, g2_ref, wr_ref,
                rb_ref, wsg_f32, wsu_f32, wsd_f32, h2_ref, base_ref, gates_ref,
                wsb_ref, wout_ref, wsgu_ref, wsd_ref):
    @pl.when(pl.program_id(0) == 0)
    def _():
        wsb_ref[...] = wsb_f32[...].astype(BF16)
        wout_ref[...] = wout_f32[...].astype(BF16)
        wsgu_ref[:, :SHARED_DIM] = wsg_f32[...].astype(BF16)
        wsgu_ref[:, SHARED_DIM:] = wsu_f32[...].astype(BF16)
        wsd_ref[...] = wsd_f32[...].astype(BF16)

    m = mod_ref[0]
    gate1, shift2, scale2, gate2 = m[2:3], m[3:4], m[4:5], m[5:6]
    gain2 = g2_ref[...] * (1.0 + scale2)
    rows = x_ref.shape[0] // MIX_ROW_SPLITS
    for c in range(MIX_ROW_SPLITS):
        rs = slice(c * rows, (c + 1) * rows)
        ysb = _dot(osb_ref[rs, :], wsb_ref[...])
        merged = sga_ref[rs, :] * ysb.astype(BF16) + mb_ref[rs, :]
        x1 = x_ref[rs, :] + gate1 * _dot(merged, wout_ref[...])

        r = lax.rsqrt(jnp.mean(x1 * x1, axis=-1, keepdims=True) + NORM_EPS)
        h2 = (x1 * r) * gain2 + shift2
        h_hi, h_lo = _split_bf16(h2)
        h2_ref[rs, :] = h_hi

        su = _dot(h_hi, wsgu_ref[...])
        sh = _dot((_silu(su[:, :SHARED_DIM]) * su[:, SHARED_DIM:]).astype(BF16), wsd_ref[...])
        base_ref[rs, :] = x1 + gate2 * sh

        hi_both = _dot(h_hi, wr_ref[...])
        logits = (hi_both[:, :LANES] + hi_both[:, LANES:]) + _dot(h_lo, wr_ref[:, :LANES])
        scores = jax.nn.sigmoid(logits.T[:N_EXPERTS, :])
        gates = _route(scores + rb_ref[...], scores)
        gates_pad = jnp.concatenate([gates, jnp.zeros((LANES - N_EXPERTS, rows), F32)], axis=0)
        gates_ref[rs, :] = gates_pad.T


def _mix(x2, osb, sga, mb, mod3, wsb, wout, g2, wr, rb, wsg, wsu, wsd, seq):
    tm = TM_MIX
    tok = x2.shape[0]
    per_batch = seq // tm
    row = lambda i: (i, 0)
    const2 = lambda i: (0, 0)
    return pl.pallas_call(
        _mix_kernel,
        out_shape=(jax.ShapeDtypeStruct((tok, D_MODEL), BF16),
                   jax.ShapeDtypeStruct((tok, D_MODEL), F32),
                   jax.ShapeDtypeStruct((tok, LANES), F32)),
        grid=(tok // tm,),
        in_specs=[
            pl.BlockSpec((tm, D_MODEL), row),
            pl.BlockSpec((tm, SB_WIDTH), row),
            pl.BlockSpec((tm, D_MODEL), row),
            pl.BlockSpec((tm, D_MODEL), row),
            pl.BlockSpec((1, 6, D_MODEL), lambda i: (i // per_batch, 0, 0)),
            pl.BlockSpec(wsb.shape, const2),
            pl.BlockSpec(wout.shape, const2),
            pl.BlockSpec((1, D_MODEL), const2),
            pl.BlockSpec(wr.shape, const2),
            pl.BlockSpec(rb.shape, const2),
            pl.BlockSpec(wsg.shape, const2),
            pl.BlockSpec(wsu.shape, const2),
            pl.BlockSpec(wsd.shape, const2),
        ],
        out_specs=(pl.BlockSpec((tm, D_MODEL), row),
                   pl.BlockSpec((tm, D_MODEL), row),
                   pl.BlockSpec((tm, LANES), row)),
        scratch_shapes=[pltpu.VMEM(wsb.shape, BF16),
                        pltpu.VMEM(wout.shape, BF16),
                        pltpu.VMEM((D_MODEL, 2 * SHARED_DIM), BF16),
                        pltpu.VMEM(wsd.shape, BF16)],
        compiler_params=pltpu.CompilerParams(dimension_semantics=("arbitrary",),
                                             vmem_limit_bytes=VMEM_LIMIT),
        name="mix",
    )(x2, osb, sga, mb, mod3, wsb, wout, g2, wr, rb, wsg, wsu, wsd)


def _moe_kernel(h_ref, g_ref, base_ref, mod_ref, wg_ref, wu_ref, wd_ref, o_ref, hm_ref):
    n_groups = N_EXPERTS // EXPERTS_PER_STEP
    s = pl.program_id(0)
    last = pl.num_programs(0) - 1
    group_up = jnp.minimum(s, last - 1) % n_groups
    group_dn = jnp.maximum(s - 1, 0) % n_groups
    width = EXPERTS_PER_STEP * EXPERT_DIM

    @pl.when(s == 0)
    def _():
        hm_ref[1] = jnp.zeros(hm_ref.shape[1:], BF16)

    @pl.when(group_dn == 0)
    def _():
        o_ref[...] = jnp.zeros_like(o_ref)

    wd = wd_ref[...].astype(BF16).reshape(width, D_MODEL)
    wgu = jnp.concatenate([wg_ref[j].astype(BF16) for j in range(EXPERTS_PER_STEP)]
                          + [wu_ref[j].astype(BF16) for j in range(EXPERTS_PER_STEP)], axis=1)
    rows = h_ref.shape[0] // MOE_ROW_SPLITS
    lane = lax.broadcasted_iota(jnp.int32, (rows, LANES), 1)
    rd, wr = (s + 1) % 2, s % 2
    for c in range(MOE_ROW_SPLITS):
        rs = slice(c * rows, (c + 1) * rows)
        o_ref[rs, :] += _dot(hm_ref[rd, rs, :], wd)
    for c in range(MOE_ROW_SPLITS):
        rs = slice(c * rows, (c + 1) * rows)
        up = _dot(h_ref[rs, :], wgu)
        gates = g_ref[rs, :]
        parts = []
        for j in range(EXPERTS_PER_STEP):
            a = up[:, j * EXPERT_DIM:(j + 1) * EXPERT_DIM]
            b = up[:, width + j * EXPERT_DIM:width + (j + 1) * EXPERT_DIM]
            gate = jnp.sum(jnp.where(lane == group_up * EXPERTS_PER_STEP + j, gates, 0.0),
                           axis=1, keepdims=True)
            parts.append((_silu(a) * b * gate).astype(BF16))
        hm_ref[wr, rs, :] = jnp.concatenate(parts, axis=1)

    @pl.when(jnp.logical_and(s > 0, group_dn == n_groups - 1))
    def _():
        gate2 = mod_ref[0][5:6]
        o_ref[...] = base_ref[...] + gate2 * o_ref[...]


def _moe(h2, gates, base, mod3, wg, wu, wd, seq):
    tm = TM_MOE
    tok = h2.shape[0]
    per_batch = seq // tm
    g = EXPERTS_PER_STEP
    n_groups = N_EXPERTS // g
    n_steps = (tok // tm) * n_groups + 1
    up_step = lambda s: jnp.minimum(s, n_steps - 2)
    dn_step = lambda s: jnp.maximum(s - 1, 0)
    up_tile = lambda s: (up_step(s) // n_groups, 0)
    dn_tile = lambda s: (dn_step(s) // n_groups, 0)
    return pl.pallas_call(
        _moe_kernel,
        out_shape=jax.ShapeDtypeStruct((tok, D_MODEL), F32),
        grid=(n_steps,),
        in_specs=[
            pl.BlockSpec((tm, D_MODEL), up_tile, pipeline_mode=pl.Buffered(1)),
            pl.BlockSpec((tm, LANES), up_tile, pipeline_mode=pl.Buffered(1)),
            pl.BlockSpec((tm, D_MODEL), dn_tile, pipeline_mode=pl.Buffered(1)),
            pl.BlockSpec((1, 6, D_MODEL), lambda s: (dn_step(s) // n_groups // per_batch, 0, 0)),
            pl.BlockSpec((g, D_MODEL, EXPERT_DIM), lambda s: (up_step(s) % n_groups, 0, 0)),
            pl.BlockSpec((g, D_MODEL, EXPERT_DIM), lambda s: (up_step(s) % n_groups, 0, 0)),
            pl.BlockSpec((g, EXPERT_DIM, D_MODEL), lambda s: (dn_step(s) % n_groups, 0, 0)),
        ],
        out_specs=pl.BlockSpec((tm, D_MODEL), dn_tile),
        scratch_shapes=[pltpu.VMEM((2, tm, g * EXPERT_DIM), BF16)],
        compiler_params=pltpu.CompilerParams(dimension_semantics=("arbitrary",),
                                             vmem_limit_bytes=VMEM_LIMIT),
        name="moe",
    )(h2, gates, base, mod3, wg, wu, wd)


def _layer(x2, c_pad, batch, seq, w_ada, b_ada, g_norm1, w_in, g_q, g_k, w_sb_out, w_pool_group,
           pool_scale, w_pool_out, w_out, g_norm2, w_router, router_bias,
           w_exp_gate, w_exp_up, w_exp_down, w_sh_gate, w_sh_up, w_sh_down):
    mod = _ada(c_pad, w_ada, b_ada[None, :])
    mod3 = mod[:batch].reshape(batch, 6, D_MODEL)

    heads_per_tile = MXU_TILE // HEAD_DIM
    head_avg = np.kron(np.eye(heads_per_tile, dtype=np.float32),
                       np.full((HEAD_DIM, HEAD_DIM), 1.0 / HEAD_DIM, np.float32))
    head_avg = jnp.asarray(head_avg, dtype=BF16)
    gq = (jnp.tile(g_q, SB_HEADS) * (HEAD_DIM ** -0.5 * LOG2_E))[None, :]
    gk = jnp.tile(g_k, SB_HEADS)[None, :]
    q, k, v, sga, mb = _inproj(
        x2, mod3, g_norm1[None, :], w_in, gq, gk, head_avg,
        w_pool_group, pool_scale[None, :], w_pool_out, batch, seq)

    shape3 = (batch, seq, SB_WIDTH)
    osb = _attention(q.reshape(shape3), k.reshape(shape3), v.reshape(shape3)).reshape(batch * seq, SB_WIDTH)

    wr = jnp.pad(w_router, ((0, 0), (0, LANES - N_EXPERTS)))
    wr_hi = wr.astype(BF16)
    wr_hi_lo = jnp.concatenate([wr_hi, (wr - wr_hi.astype(F32)).astype(BF16)], axis=1)
    h2, base, gates = _mix(
        x2, osb, sga, mb, mod3, w_sb_out, w_out, g_norm2[None, :],
        wr_hi_lo, router_bias[:, None], w_sh_gate, w_sh_up, w_sh_down, seq)

    return _moe(h2, gates, base, mod3, w_exp_gate, w_exp_up, w_exp_down, seq)


def kernel(x, c, w_ada, b_ada, g_norm1, w_in, g_q, g_k, w_sb_out, w_pool_group, pool_scale, w_pool_out,
           w_out, g_norm2, w_router, router_bias, w_exp_gate, w_exp_up, w_exp_down,
           w_sh_gate, w_sh_up, w_sh_down):
    batch, seq, d = x.shape
    x2 = x.reshape(batch * seq, d)
    c_pad = jnp.pad(c, ((0, 8 - batch), (0, 0)))
    for l in range(w_ada.shape[0]):
        x2 = _layer(x2, c_pad, batch, seq, w_ada[l], b_ada[l], g_norm1[l], w_in[l], g_q[l], g_k[l],
                    w_sb_out[l], w_pool_group[l], pool_scale[l], w_pool_out[l], w_out[l], g_norm2[l],
                    w_router[l], router_bias[l], w_exp_gate[l], w_exp_up[l], w_exp_down[l],
                    w_sh_gate[l], w_sh_up[l], w_sh_down[l])
    return x2.reshape(batch, seq, d)
```

```python
import functools

import jax
import jax.numpy as jnp
import numpy as np
from jax import lax
from jax.experimental import pallas as pl
from jax.experimental.pallas import tpu as pltpu

D_MODEL = 1024
SB_HEADS = 8
HEAD_DIM = 64
SB_WIDTH = SB_HEADS * HEAD_DIM
POOL_WINDOWS = (2, 4, 8, 16)
POOL_WIDTH = 512
POOL_GROUP_DIM = POOL_WIDTH // len(POOL_WINDOWS)
MAX_WIN = max(POOL_WINDOWS)
N_EXPERTS = 64
EXPERT_DIM = 128
TOP_K = 8
N_EXPERT_GROUPS = 8
TOPK_GROUPS = 4
SHARED_DIM = 256
ROUTED_SCALE = 2.5
NORM_EPS = 1e-6

LANES = 128
MXU_TILE = 256
VMEM_LIMIT = 56 * 1024 * 1024
MOE_VMEM_LIMIT = 60 * 1024 * 1024

TM_IN = 1024
TM_MIX = 1024
MIX_ROW_SPLITS = 2
IN_ROW_SPLITS = 2
TM_MOE = 2048
MOE_ROW_SPLITS = 2
EXPERTS_PER_STEP = 4
ATTN_Q = 128
ATTN_UNROLL = 4
ATTN_K = 384
ATTN_SKIP_LOG2 = 128.0
MASKED_LOGIT = -1e30
LOG2_E = 1.4426950408889634

F32 = jnp.float32
BF16 = jnp.bfloat16


def _dot(a, b):
    return jnp.dot(a, b, preferred_element_type=F32)


def _split_bf16(x):
    hi = x.astype(BF16)
    lo = (x - hi.astype(F32)).astype(BF16)
    return hi, lo


def _silu(x):
    return x * jax.nn.sigmoid(x)


def _ada_kernel(c_ref, w_ref, b_ref, o_ref):
    c = c_ref[...]
    o_ref[...] = _dot(_silu(c).astype(BF16), w_ref[...].astype(BF16)) + b_ref[...]


def _ada(c_pad, w_ada, b_ada):
    n = w_ada.shape[1]
    tn = D_MODEL
    return pl.pallas_call(
        _ada_kernel,
        out_shape=jax.ShapeDtypeStruct((c_pad.shape[0], n), F32),
        grid=(n // tn,),
        in_specs=[
            pl.BlockSpec(c_pad.shape, lambda i: (0, 0)),
            pl.BlockSpec((D_MODEL, tn), lambda i: (0, i)),
            pl.BlockSpec((1, tn), lambda i: (0, i)),
        ],
        out_specs=pl.BlockSpec((c_pad.shape[0], tn), lambda i: (0, i)),
        compiler_params=pltpu.CompilerParams(dimension_semantics=("arbitrary",),
                                             vmem_limit_bytes=VMEM_LIMIT),
        name="ada",
    )(c_pad, w_ada, b_ada)


def _in_kernel(x_ref, mod_ref, g1_ref, w_ref, gq_ref, gk_ref, bd_ref, wpg_ref, ps_ref, wpo_ref,
               q_ref, k_ref, v_ref, sga_ref, mb_ref, ubuf, w_bf, wpo_bf):
    j = pl.program_id(1)
    tm = x_ref.shape[0]

    @pl.when(jnp.logical_and(pl.program_id(0) == 0, j == 0))
    def _():
        w_bf[...] = w_ref[...].astype(BF16)
        for gi in range(len(POOL_WINDOWS)):
            lo = gi * POOL_GROUP_DIM
            a_hi, a_lo = _split_bf16(wpg_ref[gi] * ps_ref[:, lo:lo + POOL_GROUP_DIM])
            b_hi, b_lo = _split_bf16(wpo_ref[lo:lo + POOL_GROUP_DIM, :])
            wpo_bf[lo:lo + POOL_GROUP_DIM, :] = (_dot(a_hi, b_hi) + _dot(a_hi, b_lo) + _dot(a_lo, b_hi)).astype(BF16)
    m = mod_ref[0]
    shift1, scale1 = m[0:1], m[1:2]
    gain1 = g1_ref[...] * (1.0 + scale1)

    def headnorm(t, g_ref):
        sq = (t * t).astype(BF16)
        half = bd_ref.shape[1]
        ms = jnp.concatenate([_dot(sq[:, c:c + half], bd_ref[...]) for c in range(0, SB_WIDTH, half)], axis=1)
        return (t * lax.rsqrt(ms + NORM_EPS)) * g_ref[...]

    @pl.when(j == 0)
    def _():
        ubuf[0:MAX_WIN, :] = jnp.zeros((MAX_WIN, POOL_WIDTH), F32)

    @pl.when(j > 0)
    def _():
        ubuf[0:MAX_WIN, :] = ubuf[tm:tm + MAX_WIN, :]

    rows = tm // IN_ROW_SPLITS
    u0 = 3 * SB_WIDTH
    g0 = u0 + POOL_WIDTH
    for c in range(IN_ROW_SPLITS):
        r0 = c * rows
        rs = slice(r0, r0 + rows)
        x = x_ref[rs, :]
        r = lax.rsqrt(jnp.mean(x * x, axis=-1, keepdims=True) + NORM_EPS)
        hb = ((x * r) * gain1 + shift1).astype(BF16)

        def proj(lo, hi):
            return _dot(hb, w_bf[:, lo:hi])

        q_ref[rs, :] = headnorm(proj(0, SB_WIDTH), gq_ref).astype(BF16)
        k_ref[rs, :] = headnorm(proj(SB_WIDTH, 2 * SB_WIDTH), gk_ref).astype(BF16)
        v_ref[rs, :] = proj(2 * SB_WIDTH, 3 * SB_WIDTH).astype(BF16)

        u = proj(u0, u0 + POOL_WIDTH)
        ubuf[MAX_WIN + r0:MAX_WIN + r0 + rows, :] = u
        t = j * tm + r0 + lax.broadcasted_iota(jnp.int32, (rows, POOL_GROUP_DIM), 0)
        mixed = []
        for gi, w in enumerate(POOL_WINDOWS):
            lo = gi * POOL_GROUP_DIM
            ug = u[:, lo:lo + POOL_GROUP_DIM]
            s = ug
            for i in range(1, w):
                s = s + ubuf[MAX_WIN + r0 - i:MAX_WIN + r0 - i + rows, lo:lo + POOL_GROUP_DIM]
            cnt = jnp.minimum(t + 1, w).astype(F32)
            mixed.append((s / cnt - ug).astype(BF16))
        ypool = _dot(jnp.concatenate(mixed, axis=1), wpo_bf[...])

        sga_ref[rs, :] = jax.nn.sigmoid(proj(g0, g0 + D_MODEL)).astype(BF16)
        mb_ref[rs, :] = (jax.nn.sigmoid(proj(g0 + D_MODEL, g0 + 2 * D_MODEL)) * ypool).astype(BF16)


def _inproj(x2, mod3, g1, w_in, gq, gk, bd, wpg, ps, wpo, batch, seq):
    tm = TM_IN
    nj = seq // tm
    tok = x2.shape[0]
    row = lambda b, j: (b * nj + j, 0)
    const2 = lambda b, j: (0, 0)
    out_sd = lambda n: jax.ShapeDtypeStruct((tok, n), BF16)
    return pl.pallas_call(
        _in_kernel,
        out_shape=(out_sd(SB_WIDTH), out_sd(SB_WIDTH), out_sd(SB_WIDTH), out_sd(D_MODEL), out_sd(D_MODEL)),
        grid=(batch, nj),
        in_specs=[
            pl.BlockSpec((tm, D_MODEL), row),
            pl.BlockSpec((1, 6, D_MODEL), lambda b, j: (b, 0, 0)),
            pl.BlockSpec((1, D_MODEL), const2),
            pl.BlockSpec(w_in.shape, const2),
            pl.BlockSpec((1, SB_WIDTH), const2),
            pl.BlockSpec((1, SB_WIDTH), const2),
            pl.BlockSpec(bd.shape, const2),
            pl.BlockSpec(wpg.shape, lambda b, j: (0, 0, 0)),
            pl.BlockSpec((1, POOL_WIDTH), const2),
            pl.BlockSpec((POOL_WIDTH, D_MODEL), const2),
        ],
        out_specs=(
            pl.BlockSpec((tm, SB_WIDTH), row),
            pl.BlockSpec((tm, SB_WIDTH), row),
            pl.BlockSpec((tm, SB_WIDTH), row),
            pl.BlockSpec((tm, D_MODEL), row),
            pl.BlockSpec((tm, D_MODEL), row),
        ),
        scratch_shapes=[pltpu.VMEM((MAX_WIN + tm, POOL_WIDTH), F32),
                        pltpu.VMEM(w_in.shape, BF16),
                        pltpu.VMEM(wpo.shape, BF16)],
        compiler_params=pltpu.CompilerParams(dimension_semantics=("arbitrary", "arbitrary"),
                                             vmem_limit_bytes=VMEM_LIMIT),
        name="inproj",
    )(x2, mod3, g1, w_in, gq, gk, bd, wpg, ps, wpo)


def _attn_kernel(q_ref, k_ref, v_ref, o_ref, tri_ref):
    tq, tk = ATTN_Q, ATTN_K
    seq = q_ref.shape[1]
    n_pairs = q_ref.shape[2] // LANES
    row = lax.broadcasted_iota(jnp.int32, (tk, tk), 0)
    col = lax.broadcasted_iota(jnp.int32, (tk, tk), 1)
    tri_ref[...] = jnp.where(row >= col, 1.0, 0.0).astype(BF16)
    kcol = lax.broadcasted_iota(jnp.int32, (tq, tk), 1)
    col_minus_row = kcol - lax.broadcasted_iota(jnp.int32, (tq, tk), 0)
    first_head = lax.broadcasted_iota(jnp.int32, (tq, LANES), 1) < HEAD_DIM
    first_head_k = lax.broadcasted_iota(jnp.int32, (tk, LANES), 1) < HEAD_DIM

    n_heads = 2 * n_pairs

    def window(q_pairs, start, mask, carry=None, accs=None):
        zs = []
        for p in range(n_pairs):
            kb = k_ref[0, pl.ds(start, tk), p * LANES:(p + 1) * LANES]
            zs.append(lax.dot_general(q_pairs[p], kb, (((1,), (1,)), ((), ())), preferred_element_type=F32))
        z = jnp.concatenate(zs, axis=0).reshape(n_heads, tq, tk)
        z = jnp.where(mask[None], z, MASKED_LOGIT)
        fail = jnp.maximum(z, 0.0) + jnp.log2(1.0 + jnp.exp2(-jnp.abs(z)))
        spent = _dot(fail.astype(BF16).reshape(n_heads * tq, tk), tri_ref[...]).reshape(n_heads, tq, tk)
        log_w = z - spent
        total = spent[:, :, 0:1]
        if carry is not None:
            log_w = log_w - carry
            total = total + carry
        a = jnp.exp2(log_w).astype(BF16)
        new_accs = []
        for p in range(n_pairs):
            vb = v_ref[0, pl.ds(start, tk), p * LANES:(p + 1) * LANES]
            zero = jnp.zeros_like(vb)
            v_diag = jnp.concatenate([jnp.where(first_head_k, vb, zero), jnp.where(first_head_k, zero, vb)],
                                     axis=0)
            av = _dot(jnp.concatenate([a[2 * p], a[2 * p + 1]], axis=-1), v_diag)
            new_accs.append(av if accs is None else accs[p] + av)
        return total, tuple(new_accs)

    def q_body(it, _):
        blocks = []
        for u in range(ATTN_UNROLL):
            qs = pl.multiple_of((it * ATTN_UNROLL + u) * tq, tq)
            ws = pl.multiple_of(jnp.maximum(qs - (tk - tq), 0), tq)
            causal = col_minus_row < (qs - ws)
            q_pairs = []
            for p in range(n_pairs):
                q2 = q_ref[0, pl.ds(qs, tq), p * LANES:(p + 1) * LANES]
                zero = jnp.zeros_like(q2)
                q_pairs.append(jnp.concatenate([jnp.where(first_head, q2, zero),
                                                jnp.where(first_head, zero, q2)], axis=0))
            carry, accs = window(q_pairs, ws, causal)
            blocks.append((qs, ws, q_pairs, carry, accs))

        for qs, ws, q_pairs, carry, accs in blocks:
            def cond(st):
                start, carry, _ = st
                return jnp.logical_and(start > 0, jnp.min(carry) < ATTN_SKIP_LOG2)

            def body(st, q_pairs=q_pairs):
                start, carry, accs = st
                nxt = pl.multiple_of(jnp.maximum(start - tk, 0), tq)
                carry, accs = window(q_pairs, nxt, kcol < (start - nxt), carry, accs)
                return nxt, carry, accs

            _, _, accs = lax.while_loop(cond, body, (ws, carry, accs))
            o_ref[0, pl.ds(qs, tq), :] = jnp.concatenate(accs, axis=1).astype(BF16)
        return 0

    lax.fori_loop(0, seq // (tq * ATTN_UNROLL), q_body, 0)


def _attention(q3, k3, v3):
    batch, seq, width = q3.shape
    spec = pl.BlockSpec((1, seq, width), lambda b: (b, 0, 0))
    return pl.pallas_call(
        _attn_kernel,
        out_shape=jax.ShapeDtypeStruct((batch, seq, width), BF16),
        grid=(batch,),
        in_specs=[spec, spec, spec],
        out_specs=spec,
        scratch_shapes=[pltpu.VMEM((ATTN_K, ATTN_K), BF16)],
        compiler_params=pltpu.CompilerParams(dimension_semantics=("arbitrary",),
                                             vmem_limit_bytes=VMEM_LIMIT),
        name="attn",
    )(q3, k3, v3)


def _route(sel, scores):
    per_group = N_EXPERTS // N_EXPERT_GROUPS
    tm = sel.shape[1]
    neg = -jnp.inf
    tiles = [sel[per_group * g:per_group * (g + 1), :] for g in range(N_EXPERT_GROUPS)]
    sc_tiles = [scores[per_group * g:per_group * (g + 1), :] for g in range(N_EXPERT_GROUPS)]

    grp = []
    for tg in tiles:
        m1 = jnp.max(tg, axis=0, keepdims=True)
        eq = tg == m1
        n_eq = jnp.sum(jnp.where(eq, 1.0, 0.0), axis=0, keepdims=True)
        second = jnp.max(jnp.where(eq, neg, tg), axis=0, keepdims=True)
        grp.append(m1 + jnp.where(n_eq >= 2.0, m1, second))

    gtile = jnp.concatenate(grp, axis=0)
    gidx = lax.broadcasted_iota(jnp.int32, (N_EXPERT_GROUPS, tm), 0)
    rank = jnp.zeros((N_EXPERT_GROUPS, tm), F32)
    for gp in range(N_EXPERT_GROUPS):
        other = jnp.broadcast_to(grp[gp], (N_EXPERT_GROUPS, tm))
        rank = rank + jnp.where(other > gtile, 1.0, 0.0)
        rank = rank + jnp.where(jnp.logical_and(other == gtile, gidx > gp), 1.0, 0.0)
    keep = jnp.where(rank < float(TOPK_GROUPS), 1.0, 0.0)
    vals = [jnp.where(jnp.broadcast_to(keep[g:g + 1, :], (per_group, tm)) > 0.0, tiles[g], neg)
            for g in range(N_EXPERT_GROUPS)]

    sub = lax.broadcasted_iota(jnp.int32, (per_group, tm), 0).astype(F32)
    eidx = [sub + float(per_group * g) for g in range(N_EXPERT_GROUPS)]
    chosen = [jnp.zeros((per_group, tm), F32) for _ in range(N_EXPERT_GROUPS)]
    for _ in range(TOP_K):
        m = jnp.max(functools.reduce(jnp.maximum, vals), axis=0, keepdims=True)
        cand = [jnp.where(v == m, e, float(N_EXPERTS)) for v, e in zip(vals, eidx)]
        idx = jnp.min(functools.reduce(jnp.minimum, cand), axis=0, keepdims=True)
        for g in range(N_EXPERT_GROUPS):
            pick = eidx[g] == idx
            chosen[g] = jnp.where(pick, 1.0, chosen[g])
            vals[g] = jnp.where(pick, neg, vals[g])

    w = [jnp.where(c > 0.0, s, 0.0) for c, s in zip(chosen, sc_tiles)]
    denom = functools.reduce(jnp.add, [jnp.sum(t, axis=0, keepdims=True) for t in w])
    return jnp.concatenate([t / denom * ROUTED_SCALE for t in w], axis=0)


def _mix_kernel(x_ref, osb_ref, sga_ref, mb_ref, mod_ref, wsb_f32, wout_f32, g2_ref, wr_ref,
                rb_ref, wsg_f32, wsu_f32, wsd_f32, h2_ref, base_ref, gates_ref,
                wsb_ref, wout_ref, wsgu_ref, wsd_ref):
    @pl.when(pl.program_id(0) == 0)
    def _():
        wsb_ref[...] = wsb_f32[...].astype(BF16)
        wout_ref[...] = wout_f32[...].astype(BF16)
        wsgu_ref[:, :SHARED_DIM] = wsg_f32[...].astype(BF16)
        wsgu_ref[:, SHARED_DIM:] = wsu_f32[...].astype(BF16)
        wsd_ref[...] = wsd_f32[...].astype(BF16)

    m = mod_ref[0]
    gate1, shift2, scale2, gate2 = m[2:3], m[3:4], m[4:5], m[5:6]
    gain2 = g2_ref[...] * (1.0 + scale2)
    rows = x_ref.shape[0] // MIX_ROW_SPLITS
    for c in range(MIX_ROW_SPLITS):
        rs = slice(c * rows, (c + 1) * rows)
        ysb = _dot(osb_ref[rs, :], wsb_ref[...])
        merged = sga_ref[rs, :] * ysb.astype(BF16) + mb_ref[rs, :]
        x1 = x_ref[rs, :] + gate1 * _dot(merged, wout_ref[...])

        r = lax.rsqrt(jnp.mean(x1 * x1, axis=-1, keepdims=True) + NORM_EPS)
        h2 = (x1 * r) * gain2 + shift2
        h_hi, h_lo = _split_bf16(h2)
        h2_ref[rs, :] = h_hi

        su = _dot(h_hi, wsgu_ref[...])
        sh = _dot((_silu(su[:, :SHARED_DIM]) * su[:, SHARED_DIM:]).astype(BF16), wsd_ref[...])
        base_ref[rs, :] = x1 + gate2 * sh

        hi_both = _dot(h_hi, wr_ref[...])
        logits = (hi_both[:, :LANES] + hi_both[:, LANES:]) + _dot(h_lo, wr_ref[:, :LANES])
        scores = jax.nn.sigmoid(logits.T[:N_EXPERTS, :])
        gates = _route(scores + rb_ref[...], scores)
        gates_pad = jnp.concatenate([gates, jnp.zeros((LANES - N_EXPERTS, rows), F32)], axis=0)
        gates_ref[rs, :] = gates_pad.T


def _mix(x2, osb, sga, mb, mod3, wsb, wout, g2, wr, rb, wsg, wsu, wsd, seq):
    tm = TM_MIX
    tok = x2.shape[0]
    per_batch = seq // tm
    row = lambda i: (i, 0)
    const2 = lambda i: (0, 0)
    return pl.pallas_call(
        _mix_kernel,
        out_shape=(jax.ShapeDtypeStruct((tok, D_MODEL), BF16),
                   jax.ShapeDtypeStruct((tok, D_MODEL), F32),
                   jax.ShapeDtypeStruct((tok, LANES), F32)),
        grid=(tok // tm,),
        in_specs=[
            pl.BlockSpec((tm, D_MODEL), row),
            pl.BlockSpec((tm, SB_WIDTH), row),
            pl.BlockSpec((tm, D_MODEL), row),
            pl.BlockSpec((tm, D_MODEL), row),
            pl.BlockSpec((1, 6, D_MODEL), lambda i: (i // per_batch, 0, 0)),
            pl.BlockSpec(wsb.shape, const2),
            pl.BlockSpec(wout.shape, const2),
            pl.BlockSpec((1, D_MODEL), const2),
            pl.BlockSpec(wr.shape, const2),
            pl.BlockSpec(rb.shape, const2),
            pl.BlockSpec(wsg.shape, const2),
            pl.BlockSpec(wsu.shape, const2),
            pl.BlockSpec(wsd.shape, const2),
        ],
        out_specs=(pl.BlockSpec((tm, D_MODEL), row),
                   pl.BlockSpec((tm, D_MODEL), row),
                   pl.BlockSpec((tm, LANES), row)),
        scratch_shapes=[pltpu.VMEM(wsb.shape, BF16),
                        pltpu.VMEM(wout.shape, BF16),
                        pltpu.VMEM((D_MODEL, 2 * SHARED_DIM), BF16),
                        pltpu.VMEM(wsd.shape, BF16)],
        compiler_params=pltpu.CompilerParams(dimension_semantics=("arbitrary",),
                                             vmem_limit_bytes=VMEM_LIMIT),
        name="mix",
    )(x2, osb, sga, mb, mod3, wsb, wout, g2, wr, rb, wsg, wsu, wsd)


def _moe_kernel(h_hbm, g_hbm, base_hbm, mod_ref, wg_ref, wu_ref, wd_ref, o_ref,
                hm_ref, base_ref, h_buf, g_buf, base_sem, h_sem, g_sem):
    n_groups = N_EXPERTS // EXPERTS_PER_STEP
    s = pl.program_id(0)
    last = pl.num_programs(0) - 1
    up_step = jnp.minimum(s, last - 1)
    group_up = up_step % n_groups
    tile_up = up_step // n_groups
    group_dn = jnp.maximum(s - 1, 0) % n_groups
    width = EXPERTS_PER_STEP * EXPERT_DIM
    tm = base_ref.shape[0]
    n_tiles = h_hbm.shape[0] // tm
    slot = tile_up % 2

    def tile_copies(tile):
        rows = pl.ds(pl.multiple_of(tile * tm, tm), tm)
        return (pltpu.make_async_copy(h_hbm.at[rows, :], h_buf.at[tile % 2], h_sem.at[tile % 2]),
                pltpu.make_async_copy(g_hbm.at[rows, :], g_buf.at[tile % 2], g_sem.at[tile % 2]))

    def base_copy():
        tile = jnp.maximum(s - 1, 0) // n_groups
        return pltpu.make_async_copy(base_hbm.at[pl.ds(pl.multiple_of(tile * tm, tm), tm), :], base_ref, base_sem)

    @pl.when(s == 0)
    def _():
        for cp in tile_copies(0):
            cp.start()

    @pl.when(jnp.logical_and(group_up == 0, s < last))
    def _():
        for cp in tile_copies(tile_up):
            cp.wait()

        @pl.when(tile_up + 1 < n_tiles)
        def _():
            for cp in tile_copies(tile_up + 1):
                cp.start()

    @pl.when(jnp.logical_and(s > 0, group_dn == 0))
    def _():
        base_copy().start()

    @pl.when(s == 0)
    def _():
        hm_ref[1] = jnp.zeros(hm_ref.shape[1:], BF16)

    @pl.when(group_dn == 0)
    def _():
        o_ref[...] = jnp.zeros_like(o_ref)

    wd = wd_ref[...].astype(BF16).reshape(width, D_MODEL)
    wgu = jnp.concatenate([wg_ref[j].astype(BF16) for j in range(EXPERTS_PER_STEP)]
                          + [wu_ref[j].astype(BF16) for j in range(EXPERTS_PER_STEP)], axis=1)
    rows = tm // MOE_ROW_SPLITS
    lane = lax.broadcasted_iota(jnp.int32, (rows, LANES), 1)
    rd, wr = (s + 1) % 2, s % 2
    for c in range(MOE_ROW_SPLITS):
        rs = slice(c * rows, (c + 1) * rows)
        o_ref[rs, :] += _dot(hm_ref[rd, rs, :], wd)
    for c in range(MOE_ROW_SPLITS):
        rs = slice(c * rows, (c + 1) * rows)
        up = _dot(h_buf[slot, rs, :], wgu)
        gates = g_buf[slot, rs, :]
        parts = []
        for j in range(EXPERTS_PER_STEP):
            a = up[:, j * EXPERT_DIM:(j + 1) * EXPERT_DIM]
            b = up[:, width + j * EXPERT_DIM:width + (j + 1) * EXPERT_DIM]
            gate = jnp.sum(jnp.where(lane == group_up * EXPERTS_PER_STEP + j, gates, 0.0),
                           axis=1, keepdims=True)
            parts.append((_silu(a) * b * gate).astype(BF16))
        hm_ref[wr, rs, :] = jnp.concatenate(parts, axis=1)

    @pl.when(jnp.logical_and(s > 0, group_dn == n_groups - 1))
    def _():
        base_copy().wait()
        gate2 = mod_ref[0][5:6]
        o_ref[...] = base_ref[...] + gate2 * o_ref[...]


def _moe(h2, gates, base, mod3, wg, wu, wd, seq):
    tm = TM_MOE
    tok = h2.shape[0]
    per_batch = seq // tm
    g = EXPERTS_PER_STEP
    n_groups = N_EXPERTS // g
    n_steps = (tok // tm) * n_groups + 1
    up_step = lambda s: jnp.minimum(s, n_steps - 2)
    dn_step = lambda s: jnp.maximum(s - 1, 0)
    dn_tile = lambda s: (dn_step(s) // n_groups, 0)
    return pl.pallas_call(
        _moe_kernel,
        out_shape=jax.ShapeDtypeStruct((tok, D_MODEL), F32),
        grid=(n_steps,),
        in_specs=[
            pl.BlockSpec(memory_space=pl.ANY),
            pl.BlockSpec(memory_space=pl.ANY),
            pl.BlockSpec(memory_space=pl.ANY),
            pl.BlockSpec((1, 6, D_MODEL), lambda s: (dn_step(s) // n_groups // per_batch, 0, 0)),
            pl.BlockSpec((g, D_MODEL, EXPERT_DIM), lambda s: (up_step(s) % n_groups, 0, 0)),
            pl.BlockSpec((g, D_MODEL, EXPERT_DIM), lambda s: (up_step(s) % n_groups, 0, 0)),
            pl.BlockSpec((g, EXPERT_DIM, D_MODEL), lambda s: (dn_step(s) % n_groups, 0, 0)),
        ],
        out_specs=pl.BlockSpec((tm, D_MODEL), dn_tile),
        scratch_shapes=[pltpu.VMEM((2, tm, g * EXPERT_DIM), BF16),
                        pltpu.VMEM((tm, D_MODEL), F32),
                        pltpu.VMEM((2, tm, D_MODEL), BF16),
                        pltpu.VMEM((2, tm, LANES), F32),
                        pltpu.SemaphoreType.DMA(()),
                        pltpu.SemaphoreType.DMA((2,)),
                        pltpu.SemaphoreType.DMA((2,))],
        compiler_params=pltpu.CompilerParams(dimension_semantics=("arbitrary",),
                                             vmem_limit_bytes=MOE_VMEM_LIMIT),
        name="moe",
    )(h2, gates, base, mod3, wg, wu, wd)


def _layer(x2, c_pad, batch, seq, w_ada, b_ada, g_norm1, w_in, g_q, g_k, w_sb_out, w_pool_group,
           pool_scale, w_pool_out, w_out, g_norm2, w_router, router_bias,
           w_exp_gate, w_exp_up, w_exp_down, w_sh_gate, w_sh_up, w_sh_down):
    mod = _ada(c_pad, w_ada, b_ada[None, :])
    mod3 = mod[:batch].reshape(batch, 6, D_MODEL)

    heads_per_tile = MXU_TILE // HEAD_DIM
    head_avg = np.kron(np.eye(heads_per_tile, dtype=np.float32),
                       np.full((HEAD_DIM, HEAD_DIM), 1.0 / HEAD_DIM, np.float32))
    head_avg = jnp.asarray(head_avg, dtype=BF16)
    gq = (jnp.tile(g_q, SB_HEADS) * (HEAD_DIM ** -0.5 * LOG2_E))[None, :]
    gk = jnp.tile(g_k, SB_HEADS)[None, :]
    q, k, v, sga, mb = _inproj(
        x2, mod3, g_norm1[None, :], w_in, gq, gk, head_avg,
        w_pool_group, pool_scale[None, :], w_pool_out, batch, seq)

    shape3 = (batch, seq, SB_WIDTH)
    osb = _attention(q.reshape(shape3), k.reshape(shape3), v.reshape(shape3)).reshape(batch * seq, SB_WIDTH)

    wr = jnp.pad(w_router, ((0, 0), (0, LANES - N_EXPERTS)))
    wr_hi = wr.astype(BF16)
    wr_hi_lo = jnp.concatenate([wr_hi, (wr - wr_hi.astype(F32)).astype(BF16)], axis=1)
    h2, base, gates = _mix(
        x2, osb, sga, mb, mod3, w_sb_out, w_out, g_norm2[None, :],
        wr_hi_lo, router_bias[:, None], w_sh_gate, w_sh_up, w_sh_down, seq)

    return _moe(h2, gates, base, mod3, w_exp_gate, w_exp_up, w_exp_down, seq)


def kernel(x, c, w_ada, b_ada, g_norm1, w_in, g_q, g_k, w_sb_out, w_pool_group, pool_scale, w_pool_out,
           w_out, g_norm2, w_router, router_bias, w_exp_gate, w_exp_up, w_exp_down,
           w_sh_gate, w_sh_up, w_sh_down):
    batch, seq, d = x.shape
    x2 = x.reshape(batch * seq, d)
    c_pad = jnp.pad(c, ((0, 8 - batch), (0, 0)))
    for l in range(w_ada.shape[0]):
        x2 = _layer(x2, c_pad, batch, seq, w_ada[l], b_ada[l], g_norm1[l], w_in[l], g_q[l], g_k[l],
                    w_sb_out[l], w_pool_group[l], pool_scale[l], w_pool_out[l], w_out[l], g_norm2[l],
                    w_router[l], router_bias[l], w_exp_gate[l], w_exp_up[l], w_exp_down[l],
                    w_sh_gate[l], w_sh_up[l], w_sh_down[l])
    return x2.reshape(batch, seq, d)
```
